```python
import math
import jax, jax.numpy as jnp
from jax import lax
import numpy as np

D_MODEL = 2048
BATCH = 4
SEQ = 8192
DEPTH = 4

N_MIXERS = 3
DEEPNORM_ALPHA = (2.0 * DEPTH) ** 0.25
DEEPNORM_BETA = (8.0 * DEPTH) ** -0.25
LN_EPS = 1e-5
RMS_EPS = 1e-6

FFN_DIM = 5632

GDN_QK_HEADS = 16
GDN_V_HEADS = 32
GDN_HEAD_DIM = 128
GDN_CONV = 4
GDN_CHUNK = 64
GDN_QK_DIM = GDN_QK_HEADS * GDN_HEAD_DIM
GDN_V_DIM = GDN_V_HEADS * GDN_HEAD_DIM
GDN_CONV_DIM = 2 * GDN_QK_DIM + GDN_V_DIM
GDN_IN_DIM = GDN_CONV_DIM + GDN_V_DIM + 2 * GDN_V_HEADS

GLA_HEADS = 4
GLA_K_DIM = D_MODEL // 2
GLA_V_DIM = D_MODEL
GLA_HEAD_K = GLA_K_DIM // GLA_HEADS
GLA_HEAD_V = GLA_V_DIM // GLA_HEADS
GLA_GATE_RANK = 16
GLA_GATE_TAU = 16.0
GLA_CHUNK = 64
GLA_IN_DIM = 2 * GLA_K_DIM + 2 * GLA_V_DIM + GLA_GATE_RANK

S5_GROUP = 16
S5_GROUPS = D_MODEL // S5_GROUP
S5_STATE = 64
S5_CHUNK = 512

N_GDN = (DEPTH + 2) // 3
N_GLA = (DEPTH + 1) // 3
N_S5 = DEPTH // 3

kernel_name = "hybrid_gdn_gla_s5_macaron_deepnorm"

F32 = jnp.float32


def layer_norm(x, g, b):
    xf = x.astype(F32)
    mu = jnp.mean(xf, -1, keepdims=True)
    xc = xf - mu
    var = jnp.mean(xc * xc, -1, keepdims=True)
    return (xc * lax.rsqrt(var + LN_EPS) * g.astype(F32) + b.astype(F32)).astype(x.dtype)


def rms_norm(x, w):
    xf = x.astype(F32)
    return xf * lax.rsqrt(jnp.mean(xf * xf, -1, keepdims=True) + RMS_EPS) * w.astype(F32)


def l2_normalize(x):
    xf = x.astype(F32)
    return xf * lax.rsqrt(jnp.sum(xf * xf, -1, keepdims=True) + RMS_EPS)


def swiglu_ffn(x, w_up, w_down):
    gate, up = jnp.split(x @ w_up, 2, axis=-1)
    return (jax.nn.silu(gate) * up) @ w_down


def causal_depthwise_conv(x, w):
    k = w.shape[0]
    return lax.conv_general_dilated(
        x, w[:, None, :].astype(x.dtype), window_strides=(1,), padding=[(k - 1, 0)],
        dimension_numbers=("NWC", "WIO", "NWC"), feature_group_count=x.shape[-1])


def to_chunks(t, c):
    b, s = t.shape[:2]
    t = t.reshape((b, s // c, c) + t.shape[2:])
    return t.transpose((1, 0, 3, 2) + tuple(range(4, t.ndim)))


def from_chunks(t):
    n, b, h, c = t.shape[:4]
    t = t.transpose((1, 0, 3, 2) + tuple(range(4, t.ndim)))
    return t.reshape((b, n * c, h) + t.shape[4:])


def chunk_gated_delta_rule(q, k, v, g, beta):
    bsz, _, hk, dk = q.shape
    h, dv = v.shape[2], v.shape[3]
    rep = h // hk
    c = GDN_CHUNK
    causal = jnp.tril(jnp.ones((c, c), bool))
    strict = jnp.tril(jnp.ones((c, c), bool), -1)

    def step(state, inp):
        qc, kc, vc, gc, bc = inp
        qc = jnp.repeat(qc, rep, axis=1)
        kc = jnp.repeat(kc, rep, axis=1)
        gcum = jnp.cumsum(gc, axis=-1)
        decay = jnp.exp(jnp.where(causal, gcum[..., :, None] - gcum[..., None, :], -jnp.inf))
        kb = kc * bc[..., None]
        lmat = jnp.where(strict, jnp.einsum("bhid,bhjd->bhij", kb, kc) * decay, 0.0)
        rhs = jnp.concatenate([vc * bc[..., None], kb * jnp.exp(gcum)[..., None]], axis=-1)
        sol = lax.linalg.triangular_solve(lmat, rhs, left_side=True, lower=True, unit_diagonal=True)
        u, w = sol[..., :dv], sol[..., dv:]
        v_new = u - jnp.einsum("bhck,bhkv->bhcv", w, state)
        attn = jnp.einsum("bhid,bhjd->bhij", qc, kc) * decay
        o = (jnp.einsum("bhck,bhkv->bhcv", qc * jnp.exp(gcum)[..., None], state)
             + jnp.einsum("bhij,bhjv->bhiv", attn, v_new))
        glast = gcum[..., -1]
        state = (state * jnp.exp(glast)[..., None, None]
                 + jnp.einsum("bhck,bhcv->bhkv", kc * jnp.exp(glast[..., None] - gcum)[..., None], v_new))
        return state, o

    s0 = jnp.zeros((bsz, h, dk, dv), F32)
    _, o = lax.scan(step, s0, (to_chunks(q, c), to_chunks(k, c), to_chunks(v, c),
                               to_chunks(g, c), to_chunks(beta, c)))
    return from_chunks(o)


def gdn_mixer(x, w_in, conv_w, a_log, dt_bias, norm_w, w_out):
    bsz, s, _ = x.shape
    proj = x @ w_in
    qkv, z, b_in, a_in = jnp.split(
        proj, [GDN_CONV_DIM, GDN_CONV_DIM + GDN_V_DIM, GDN_CONV_DIM + GDN_V_DIM + GDN_V_HEADS], axis=-1)
    qkv = jax.nn.silu(causal_depthwise_conv(qkv, conv_w))
    q, k, v = jnp.split(qkv, [GDN_QK_DIM, 2 * GDN_QK_DIM], axis=-1)
    q = l2_normalize(q.reshape(bsz, s, GDN_QK_HEADS, GDN_HEAD_DIM)) * (GDN_HEAD_DIM ** -0.5)
    k = l2_normalize(k.reshape(bsz, s, GDN_QK_HEADS, GDN_HEAD_DIM))
    v = v.reshape(bsz, s, GDN_V_HEADS, GDN_HEAD_DIM).astype(F32)
    beta = jax.nn.sigmoid(b_in.astype(F32))
    g = -jnp.exp(a_log.astype(F32)) * jax.nn.softplus(a_in.astype(F32) + dt_bias.astype(F32))
    o = chunk_gated_delta_rule(q, k, v, g, beta)
    o = rms_norm(o, norm_w) * jax.nn.silu(z.astype(F32).reshape(bsz, s, GDN_V_HEADS, GDN_HEAD_DIM))
    return o.reshape(bsz, s, GDN_V_DIM).astype(x.dtype) @ w_out


def chunk_gla(q, k, v, log_f):
    bsz, _, h, dk = q.shape
    dv = v.shape[-1]
    c = GLA_CHUNK
    causal = jnp.tril(jnp.ones((c, c), bool))[..., None]

    def step(state, inp):
        qc, kc, vc, fc = inp
        bcum = jnp.cumsum(fc, axis=2)
        decay = jnp.exp(jnp.where(causal, bcum[:, :, :, None, :] - bcum[:, :, None, :, :], -jnp.inf))
        attn = jnp.einsum("bhik,bhjk,bhijk->bhij", qc, kc, decay)
        o = (jnp.einsum("bhck,bhkv->bhcv", qc * jnp.exp(bcum), state)
             + jnp.einsum("bhij,bhjv->bhiv", attn, vc))
        blast = bcum[:, :, -1]
        state = (state * jnp.exp(blast)[..., None]
                 + jnp.einsum("bhck,bhcv->bhkv", kc * jnp.exp(blast[:, :, None] - bcum), vc))
        return state, o

    s0 = jnp.zeros((bsz, h, dk, dv), F32)
    _, o = lax.scan(step, s0, (to_chunks(q, c), to_chunks(k, c), to_chunks(v, c), to_chunks(log_f, c)))
    return from_chunks(o)


def gla_mixer(x, w_in, w_gate, gate_bias, norm_w, w_out):
    bsz, s, _ = x.shape
    proj = x @ w_in
    q, k, v, r, gl = jnp.split(
        proj, [GLA_K_DIM, 2 * GLA_K_DIM, 2 * GLA_K_DIM + GLA_V_DIM, 2 * GLA_K_DIM + 2 * GLA_V_DIM], axis=-1)
    log_f = jax.nn.log_sigmoid((gl @ w_gate).astype(F32) + gate_bias.astype(F32)) / GLA_GATE_TAU
    q = q.reshape(bsz, s, GLA_HEADS, GLA_HEAD_K).astype(F32) * (GLA_HEAD_K ** -0.5)
    k = k.reshape(bsz, s, GLA_HEADS, GLA_HEAD_K).astype(F32)
    v = v.reshape(bsz, s, GLA_HEADS, GLA_HEAD_V).astype(F32)
    o = chunk_gla(q, k, v, log_f.reshape(bsz, s, GLA_HEADS, GLA_HEAD_K))
    o = rms_norm(o, norm_w) * jax.nn.silu(r.astype(F32).reshape(bsz, s, GLA_HEADS, GLA_HEAD_V))
    return o.reshape(bsz, s, GLA_V_DIM).astype(x.dtype) @ w_out


def _linear_recurrence_op(e1, e2):
    a1, b1 = e1
    a2, b2 = e2
    return a1 * a2, a2 * b1 + b2


def s5_mixer(x, a_re, a_im, b_re, b_im, c_re, c_im, d_skip, log_dt, w_glu):
    bsz, s, dm = x.shape
    lc = math.gcd(s, S5_CHUNK)
    u = x.astype(F32).reshape(bsz, s // lc, lc, S5_GROUPS, S5_GROUP).transpose(1, 2, 0, 3, 4)
    lam = lax.complex(a_re.astype(F32), a_im.astype(F32))
    dt = jnp.exp(log_dt.astype(F32))[:, None]
    a_bar = jnp.exp(lam * dt)
    b_bar = ((a_bar - 1.0) / lam)[..., None] * lax.complex(b_re.astype(F32), b_im.astype(F32))
    c_mat = lax.complex(c_re.astype(F32), c_im.astype(F32))
    d = d_skip.astype(F32).reshape(S5_GROUPS, S5_GROUP)

    def step(h, uc):
        bu = jnp.einsum("tbgh,gph->tbgp", uc.astype(jnp.complex64), b_bar)
        a = jnp.broadcast_to(a_bar, bu.shape)
        a_cum, b_cum = lax.associative_scan(_linear_recurrence_op, (a, bu), axis=0)
        states = b_cum + a_cum * h[None]
        y = jnp.real(jnp.einsum("tbgp,ghp->tbgh", states, c_mat)) + d * uc
        return states[-1], y

    h0 = jnp.zeros((bsz, S5_GROUPS, S5_STATE), jnp.complex64)
    _, y = lax.scan(step, h0, u)
    y = jax.nn.gelu(y.transpose(2, 0, 1, 3, 4).reshape(bsz, s, dm)).astype(x.dtype)
    val, gate = jnp.split(y @ w_glu, 2, axis=-1)
    return val * jax.nn.sigmoid(gate)


def setup_inputs(seed: int = 0) -> dict:
    key = jax.random.key(seed)
    ks = jax.random.split(key, 24)

    def nrm(k, shape, scale):
        return jax.random.normal(k, shape, F32) * scale

    beta = DEEPNORM_BETA
    x = nrm(ks[0], (BATCH, SEQ, D_MODEL), 1.0)
    ln_g = 1.0 + nrm(ks[1], (DEPTH, 3, D_MODEL), 0.02)
    ln_b = nrm(ks[2], (DEPTH, 3, D_MODEL), 0.02)
    ffn_w_up = nrm(ks[3], (DEPTH, 2, D_MODEL, 2 * FFN_DIM), D_MODEL ** -0.5)
    ffn_w_down = nrm(ks[4], (DEPTH, 2, FFN_DIM, D_MODEL), beta * FFN_DIM ** -0.5)

    gdn_w_in = nrm(ks[5], (N_GDN, D_MODEL, GDN_IN_DIM), D_MODEL ** -0.5)
    gdn_conv_w = nrm(ks[6], (N_GDN, GDN_CONV, GDN_CONV_DIM), GDN_CONV ** -0.5)
    gdn_a_log = jnp.log(jax.random.uniform(ks[7], (N_GDN, GDN_V_HEADS), F32, 1.0, 16.0))
    dt = jnp.exp(jax.random.uniform(ks[8], (N_GDN, GDN_V_HEADS), F32, math.log(1e-3), math.log(1e-1)))
    gdn_dt_bias = dt + jnp.log(-jnp.expm1(-dt))
    gdn_norm_w = 1.0 + nrm(ks[9], (N_GDN, GDN_HEAD_DIM), 0.02)
    gdn_w_out = nrm(ks[10], (N_GDN, GDN_V_DIM, D_MODEL), beta * GDN_V_DIM ** -0.5)

    gla_w_in = nrm(ks[11], (N_GLA, D_MODEL, GLA_IN_DIM), D_MODEL ** -0.5)
    gla_w_gate = nrm(ks[12], (N_GLA, GLA_GATE_RANK, GLA_K_DIM), GLA_GATE_RANK ** -0.5)
    gla_gate_bias = nrm(ks[13], (N_GLA, GLA_K_DIM), 0.1)
    gla_norm_w = 1.0 + nrm(ks[14], (N_GLA, GLA_HEAD_V), 0.02)
    gla_w_out = nrm(ks[15], (N_GLA, GLA_V_DIM, D_MODEL), beta * GLA_V_DIM ** -0.5)

    s5_a_re = -0.5 + nrm(ks[16], (N_S5, S5_GROUPS, S5_STATE), 0.01)
    s5_a_im = jnp.broadcast_to(jnp.pi * jnp.arange(S5_STATE, dtype=F32), (N_S5, S5_GROUPS, S5_STATE))
    s5_b_re = nrm(ks[17], (N_S5, S5_GROUPS, S5_STATE, S5_GROUP), (2 * S5_GROUP) ** -0.5)
    s5_b_im = nrm(ks[18], (N_S5, S5_GROUPS, S5_STATE, S5_GROUP), (2 * S5_GROUP) ** -0.5)
    s5_c_re = nrm(ks[19], (N_S5, S5_GROUPS, S5_GROUP, S5_STATE), (2 * S5_STATE) ** -0.5)
    s5_c_im = nrm(ks[20], (N_S5, S5_GROUPS, S5_GROUP, S5_STATE), (2 * S5_STATE) ** -0.5)
    s5_d = nrm(ks[21], (N_S5, D_MODEL), 1.0)
    s5_log_dt = jax.random.uniform(ks[22], (N_S5, S5_GROUPS), F32, math.log(1e-3), math.log(1e-1))
    glu = nrm(ks[23], (N_S5, D_MODEL, 2 * D_MODEL), D_MODEL ** -0.5)
    s5_w_glu = jnp.concatenate([glu[..., :D_MODEL] * beta, glu[..., D_MODEL:]], axis=-1)

    return {"x": x, "ln_g": ln_g, "ln_b": ln_b, "ffn_w_up": ffn_w_up, "ffn_w_down": ffn_w_down,
            "gdn_w_in": gdn_w_in, "gdn_conv_w": gdn_conv_w, "gdn_a_log": gdn_a_log,
            "gdn_dt_bias": gdn_dt_bias, "gdn_norm_w": gdn_norm_w, "gdn_w_out": gdn_w_out,
            "gla_w_in": gla_w_in, "gla_w_gate": gla_w_gate, "gla_gate_bias": gla_gate_bias,
            "gla_norm_w": gla_norm_w, "gla_w_out": gla_w_out,
            "s5_a_re": s5_a_re, "s5_a_im": s5_a_im, "s5_b_re": s5_b_re, "s5_b_im": s5_b_im,
            "s5_c_re": s5_c_re, "s5_c_im": s5_c_im, "s5_d": s5_d, "s5_log_dt": s5_log_dt,
            "s5_w_glu": s5_w_glu}


def reference(x, ln_g, ln_b, ffn_w_up, ffn_w_down,
              gdn_w_in, gdn_conv_w, gdn_a_log, gdn_dt_bias, gdn_norm_w, gdn_w_out,
              gla_w_in, gla_w_gate, gla_gate_bias, gla_norm_w, gla_w_out,
              s5_a_re, s5_a_im, s5_b_re, s5_b_im, s5_c_re, s5_c_im, s5_d, s5_log_dt, s5_w_glu):
    for i in range(DEPTH):
        x = layer_norm(DEEPNORM_ALPHA * x + 0.5 * swiglu_ffn(x, ffn_w_up[i, 0], ffn_w_down[i, 0]),
                       ln_g[i, 0], ln_b[i, 0])
        kind, j = i % N_MIXERS, i // N_MIXERS
        if kind == 0:
            m = gdn_mixer(x, gdn_w_in[j], gdn_conv_w[j], gdn_a_log[j], gdn_dt_bias[j], gdn_norm_w[j], gdn_w_out[j])
        elif kind == 1:
            m = gla_mixer(x, gla_w_in[j], gla_w_gate[j], gla_gate_bias[j], gla_norm_w[j], gla_w_out[j])
        else:
            m = s5_mixer(x, s5_a_re[j], s5_a_im[j], s5_b_re[j], s5_b_im[j], s5_c_re[j], s5_c_im[j],
                         s5_d[j], s5_log_dt[j], s5_w_glu[j])
        x = layer_norm(DEEPNORM_ALPHA * x + m, ln_g[i, 1], ln_b[i, 1])
        x = layer_norm(DEEPNORM_ALPHA * x + 0.5 * swiglu_ffn(x, ffn_w_up[i, 1], ffn_w_down[i, 1]),
                       ln_g[i, 2], ln_b[i, 2])
    return x
```

```python
import functools
import math

import jax
import jax.numpy as jnp
from jax import lax
from jax.experimental import pallas as pl
from jax.experimental.pallas import tpu as pltpu

F32 = jnp.float32
BF16 = jnp.bfloat16

D_MODEL = 2048
DEPTH = 4
ALPHA = (2.0 * DEPTH) ** 0.25
LN_EPS = 1e-5
RMS_EPS = 1e-6
FFN_DIM = 5632

GDN_QK_HEADS = 16
GDN_V_HEADS = 32
GDN_HEAD_DIM = 128
GDN_CONV = 4
GDN_CHUNK = 64
GDN_QK_DIM = GDN_QK_HEADS * GDN_HEAD_DIM
GDN_V_DIM = GDN_V_HEADS * GDN_HEAD_DIM
GDN_CONV_DIM = 2 * GDN_QK_DIM + GDN_V_DIM

GLA_HEADS = 4
GLA_K_DIM = D_MODEL // 2
GLA_V_DIM = D_MODEL
GLA_HEAD_K = GLA_K_DIM // GLA_HEADS
GLA_HEAD_V = GLA_V_DIM // GLA_HEADS
GLA_GATE_RANK = 16
GLA_GATE_TAU = 16.0
GLA_CHUNK = 64

S5_GROUP = 16
S5_GROUPS = D_MODEL // S5_GROUP
S5_STATE = 64
S5_CHUNK = 512

VMEM_LIMIT = 48 * 1024 * 1024


def _layer_norm(y, g, b):
    mu = jnp.mean(y, axis=-1, keepdims=True)
    yc = y - mu
    var = jnp.mean(yc * yc, axis=-1, keepdims=True)
    return yc * lax.rsqrt(var + LN_EPS) * g + b


def _ffn_kernel(x_ref, wg_ref, wu_ref, wd_ref, g_ref, b_ref, o_ref, xb_ref, acc_ref):
    f = pl.program_id(1)

    @pl.when(f == 0)
    def _():
        xb_ref[...] = x_ref[...].astype(BF16)
        acc_ref[...] = jnp.zeros_like(acc_ref)

    xb = xb_ref[...]
    gate = jnp.dot(xb, wg_ref[...], preferred_element_type=F32)
    up = jnp.dot(xb, wu_ref[...], preferred_element_type=F32)
    act = (gate * jax.nn.sigmoid(gate) * up).astype(BF16)
    acc_ref[...] += jnp.dot(act, wd_ref[...], preferred_element_type=F32)

    @pl.when(f == pl.num_programs(1) - 1)
    def _():
        y = ALPHA * x_ref[...] + 0.5 * acc_ref[...]
        o_ref[...] = _layer_norm(y, g_ref[...], b_ref[...])


def _ffn_ln(x, w_up, w_down, g, b, *, tm=512, tf=512):
    t, d = x.shape
    fdim = w_down.shape[0]
    nf = fdim // tf
    return pl.pallas_call(
        _ffn_kernel,
        out_shape=jax.ShapeDtypeStruct((t, d), F32),
        grid=(t // tm, nf),
        in_specs=[
            pl.BlockSpec((tm, d), lambda i, f: (i, 0)),
            pl.BlockSpec((d, tf), lambda i, f: (0, f)),
            pl.BlockSpec((d, tf), lambda i, f: (0, f + nf)),
            pl.BlockSpec((tf, d), lambda i, f: (f, 0)),
            pl.BlockSpec((1, d), lambda i, f: (0, 0)),
            pl.BlockSpec((1, d), lambda i, f: (0, 0)),
        ],
        out_specs=pl.BlockSpec((tm, d), lambda i, f: (i, 0)),
        scratch_shapes=[pltpu.VMEM((tm, d), BF16), pltpu.VMEM((tm, d), F32)],
        compiler_params=pltpu.CompilerParams(
            dimension_semantics=("parallel", "arbitrary"), vmem_limit_bytes=VMEM_LIMIT),
        name="ffn_ln",
    )(x, w_up, w_up, w_down, g.reshape(1, d), b.reshape(1, d))


def _mm_kernel(x_ref, w_ref, o_ref):
    o_ref[...] = jnp.dot(x_ref[...].astype(BF16), w_ref[...], preferred_element_type=F32)


def _matmul(x, w, *, tm=1024, tn=1024):
    t, k = x.shape
    n = w.shape[1]
    tn = min(tn, n)
    return pl.pallas_call(
        _mm_kernel,
        out_shape=jax.ShapeDtypeStruct((t, n), F32),
        grid=(t // tm, n // tn),
        in_specs=[
            pl.BlockSpec((tm, k), lambda i, j: (i, 0)),
            pl.BlockSpec((k, tn), lambda i, j: (0, j)),
        ],
        out_specs=pl.BlockSpec((tm, tn), lambda i, j: (i, j)),
        compiler_params=pltpu.CompilerParams(
            dimension_semantics=("parallel", "arbitrary"), vmem_limit_bytes=VMEM_LIMIT),
        name="proj",
    )(x, w)


def _mm_res_ln_kernel(a_ref, w_ref, r_ref, g_ref, b_ref, o_ref, acc_ref):
    k = pl.program_id(1)

    @pl.when(k == 0)
    def _():
        acc_ref[...] = jnp.zeros_like(acc_ref)

    acc_ref[...] += jnp.dot(a_ref[...], w_ref[...], preferred_element_type=F32)

    @pl.when(k == pl.num_programs(1) - 1)
    def _():
        y = ALPHA * r_ref[...] + acc_ref[...]
        o_ref[...] = _layer_norm(y, g_ref[...], b_ref[...])


def _matmul_res_ln(a, w, res, g, b, *, tm=512, tk=1024):
    t, kdim = a.shape
    d = w.shape[1]
    return pl.pallas_call(
        _mm_res_ln_kernel,
        out_shape=jax.ShapeDtypeStruct((t, d), F32),
        grid=(t // tm, kdim // tk),
        in_specs=[
            pl.BlockSpec((tm, tk), lambda i, k: (i, k)),
            pl.BlockSpec((tk, d), lambda i, k: (k, 0)),
            pl.BlockSpec((tm, d), lambda i, k: (i, 0)),
            pl.BlockSpec((1, d), lambda i, k: (0, 0)),
            pl.BlockSpec((1, d), lambda i, k: (0, 0)),
        ],
        out_specs=pl.BlockSpec((tm, d), lambda i, k: (i, 0)),
        scratch_shapes=[pltpu.VMEM((tm, d), F32)],
        compiler_params=pltpu.CompilerParams(
            dimension_semantics=("parallel", "arbitrary"), vmem_limit_bytes=VMEM_LIMIT),
        name="out_proj_ln",
    )(a, w, res, g.reshape(1, d), b.reshape(1, d))


def _glu_res_ln_kernel(a_ref, wv_ref, wg_ref, r_ref, g_ref, b_ref, o_ref):
    a = a_ref[...]
    val = jnp.dot(a, wv_ref[...], preferred_element_type=F32)
    gate = jnp.dot(a, wg_ref[...], preferred_element_type=F32)
    y = ALPHA * r_ref[...] + val * jax.nn.sigmoid(gate)
    o_ref[...] = _layer_norm(y, g_ref[...], b_ref[...])


def _glu_res_ln(a, w_glu, res, g, b, *, tm=256):
    t, kdim = a.shape
    d = res.shape[1]
    return pl.pallas_call(
        _glu_res_ln_kernel,
        out_shape=jax.ShapeDtypeStruct((t, d), F32),
        grid=(t // tm,),
        in_specs=[
            pl.BlockSpec((tm, kdim), lambda i: (i, 0)),
            pl.BlockSpec((kdim, d), lambda i: (0, 0)),
            pl.BlockSpec((kdim, d), lambda i: (0, 1)),
            pl.BlockSpec((tm, d), lambda i: (i, 0)),
            pl.BlockSpec((1, d), lambda i: (0, 0)),
            pl.BlockSpec((1, d), lambda i: (0, 0)),
        ],
        out_specs=pl.BlockSpec((tm, d), lambda i: (i, 0)),
        compiler_params=pltpu.CompilerParams(
            dimension_semantics=("parallel",), vmem_limit_bytes=VMEM_LIMIT),
        name="glu_ln",
    )(a, w_glu, w_glu, res, g.reshape(1, d), b.reshape(1, d))


def _to_chunks(t, c):
    b, s = t.shape[:2]
    t = t.reshape((b, s // c, c) + t.shape[2:])
    return t.transpose((1, 0, 3, 2) + tuple(range(4, t.ndim)))


def _from_chunks(t):
    n, b, h, c = t.shape[:4]
    t = t.transpose((1, 0, 3, 2) + tuple(range(4, t.ndim)))
    return t.reshape((b, n * c, h) + t.shape[4:])


def _gdn_core_jax(proj, conv_w, a_log, dt_bias, norm_w, bsz, s):
    proj = proj.reshape(bsz, s, -1)
    qkv, z, b_in, a_in = jnp.split(
        proj, [GDN_CONV_DIM, GDN_CONV_DIM + GDN_V_DIM, GDN_CONV_DIM + GDN_V_DIM + GDN_V_HEADS], axis=-1)
    xp = jnp.pad(qkv, ((0, 0), (GDN_CONV - 1, 0), (0, 0)))
    qkv = sum(xp[:, j:j + s, :] * conv_w[j] for j in range(GDN_CONV))
    qkv = jax.nn.silu(qkv)
    q, k, v = jnp.split(qkv, [GDN_QK_DIM, 2 * GDN_QK_DIM], axis=-1)

    def l2n(x):
        return x * lax.rsqrt(jnp.sum(x * x, -1, keepdims=True) + RMS_EPS)

    q = l2n(q.reshape(bsz, s, GDN_QK_HEADS, GDN_HEAD_DIM)) * (GDN_HEAD_DIM ** -0.5)
    k = l2n(k.reshape(bsz, s, GDN_QK_HEADS, GDN_HEAD_DIM))
    v = v.reshape(bsz, s, GDN_V_HEADS, GDN_HEAD_DIM)
    beta = jax.nn.sigmoid(b_in)
    g = -jnp.exp(a_log) * jax.nn.softplus(a_in + dt_bias)
    h, dv, dk = GDN_V_HEADS, GDN_HEAD_DIM, GDN_HEAD_DIM
    rep = h // GDN_QK_HEADS
    c = GDN_CHUNK
    causal = jnp.tril(jnp.ones((c, c), bool))
    strict = jnp.tril(jnp.ones((c, c), bool), -1)
    hp = lax.Precision.HIGHEST

    def step(state, inp):
        qc, kc, vc, gc, bc = inp
        qc = jnp.repeat(qc, rep, axis=1)
        kc = jnp.repeat(kc, rep, axis=1)
        gcum = jnp.cumsum(gc, axis=-1)
        decay = jnp.exp(jnp.where(causal, gcum[..., :, None] - gcum[..., None, :], -jnp.inf))
        kb = kc * bc[..., None]
        lmat = jnp.where(strict, jnp.einsum("bhid,bhjd->bhij", kb, kc) * decay, 0.0)
        rhs = jnp.concatenate([vc * bc[..., None], kb * jnp.exp(gcum)[..., None]], axis=-1)
        sol = lax.linalg.triangular_solve(lmat, rhs, left_side=True, lower=True, unit_diagonal=True)
        u, w = sol[..., :dv], sol[..., dv:]
        v_new = u - jnp.einsum("bhck,bhkv->bhcv", w, state)
        attn = jnp.einsum("bhid,bhjd->bhij", qc, kc) * decay
        o = (jnp.einsum("bhck,bhkv->bhcv", qc * jnp.exp(gcum)[..., None], state)
             + jnp.einsum("bhij,bhjv->bhiv", attn, v_new))
        glast = gcum[..., -1]
        state = (state * jnp.exp(glast)[..., None, None]
                 + jnp.einsum("bhck,bhcv->bhkv", kc * jnp.exp(glast[..., None] - gcum)[..., None], v_new))
        return state, o

    s0 = jnp.zeros((bsz, h, dk, dv), F32)
    _, o = lax.scan(step, s0, (_to_chunks(q, c), _to_chunks(k, c), _to_chunks(v, c),
                               _to_chunks(g, c), _to_chunks(beta, c)))
    o = _from_chunks(o)
    o = o * lax.rsqrt(jnp.mean(o * o, -1, keepdims=True) + RMS_EPS) * norm_w
    o = o * jax.nn.silu(z.reshape(bsz, s, GDN_V_HEADS, GDN_HEAD_DIM))
    return o.reshape(bsz * s, GDN_V_DIM).astype(BF16)


def _gla_core_jax(proj, w_gate, gate_bias, norm_w, bsz, s):
    proj = proj.reshape(bsz, s, -1)
    q, k, v, r, gl = jnp.split(
        proj, [GLA_K_DIM, 2 * GLA_K_DIM, 2 * GLA_K_DIM + GLA_V_DIM, 2 * GLA_K_DIM + 2 * GLA_V_DIM], axis=-1)
    log_f = jax.nn.log_sigmoid(gl @ w_gate + gate_bias) / GLA_GATE_TAU
    q = q.reshape(bsz, s, GLA_HEADS, GLA_HEAD_K) * (GLA_HEAD_K ** -0.5)
    k = k.reshape(bsz, s, GLA_HEADS, GLA_HEAD_K)
    v = v.reshape(bsz, s, GLA_HEADS, GLA_HEAD_V)
    log_f = log_f.reshape(bsz, s, GLA_HEADS, GLA_HEAD_K)
    c = GLA_CHUNK
    causal = jnp.tril(jnp.ones((c, c), bool))[..., None]

    def step(state, inp):
        qc, kc, vc, fc = inp
        bcum = jnp.cumsum(fc, axis=2)
        decay = jnp.exp(jnp.where(causal, bcum[:, :, :, None, :] - bcum[:, :, None, :, :], -jnp.inf))
        attn = jnp.einsum("bhik,bhjk,bhijk->bhij", qc, kc, decay)
        o = (jnp.einsum("bhck,bhkv->bhcv", qc * jnp.exp(bcum), state)
             + jnp.einsum("bhij,bhjv->bhiv", attn, vc))
        blast = bcum[:, :, -1]
        state = (state * jnp.exp(blast)[..., None]
                 + jnp.einsum("bhck,bhcv->bhkv", kc * jnp.exp(blast[:, :, None] - bcum), vc))
        return state, o

    s0 = jnp.zeros((bsz, GLA_HEADS, GLA_HEAD_K, GLA_HEAD_V), F32)
    _, o = lax.scan(step, s0, (_to_chunks(q, c), _to_chunks(k, c), _to_chunks(v, c), _to_chunks(log_f, c)))
    o = _from_chunks(o)
    o = o * lax.rsqrt(jnp.mean(o * o, -1, keepdims=True) + RMS_EPS) * norm_w
    o = o * jax.nn.silu(r.reshape(bsz, s, GLA_HEADS, GLA_HEAD_V))
    return o.reshape(bsz * s, GLA_V_DIM).astype(BF16)


def _s5_core_jax(x, a_re, a_im, b_re, b_im, c_re, c_im, d_skip, log_dt, bsz, s):
    x = x.reshape(bsz, s, D_MODEL)
    lc = math.gcd(s, S5_CHUNK)
    u = x.reshape(bsz, s // lc, lc, S5_GROUPS, S5_GROUP).transpose(1, 2, 0, 3, 4)
    lam = lax.complex(a_re, a_im)
    dt = jnp.exp(log_dt)[:, None]
    a_bar = jnp.exp(lam * dt)
    b_bar = ((a_bar - 1.0) / lam)[..., None] * lax.complex(b_re, b_im)
    c_mat = lax.complex(c_re, c_im)
    d = d_skip.reshape(S5_GROUPS, S5_GROUP)

    def op(e1, e2):
        a1, b1 = e1
        a2, b2 = e2
        return a1 * a2, a2 * b1 + b2

    def step(h, uc):
        bu = jnp.einsum("tbgh,gph->tbgp", uc.astype(jnp.complex64), b_bar)
        a = jnp.broadcast_to(a_bar, bu.shape)
        a_cum, b_cum = lax.associative_scan(op, (a, bu), axis=0)
        states = b_cum + a_cum * h[None]
        y = jnp.real(jnp.einsum("tbgp,ghp->tbgh", states, c_mat)) + d * uc
        return states[-1], y

    h0 = jnp.zeros((bsz, S5_GROUPS, S5_STATE), jnp.complex64)
    _, y = lax.scan(step, h0, u)
    y = jax.nn.gelu(y.transpose(2, 0, 1, 3, 4).reshape(bsz * s, D_MODEL))
    return y.astype(BF16)


def kernel(x, ln_g, ln_b, ffn_w_up, ffn_w_down, gdn_w_in, gdn_conv_w, gdn_a_log, gdn_dt_bias,
           gdn_norm_w, gdn_w_out, gla_w_in, gla_w_gate, gla_gate_bias, gla_norm_w, gla_w_out,
           s5_a_re, s5_a_im, s5_b_re, s5_b_im, s5_c_re, s5_c_im, s5_d, s5_log_dt, s5_w_glu):
    bsz, s, d = x.shape
    h = x.reshape(bsz * s, d)
    w_up = ffn_w_up.astype(BF16)
    w_down = ffn_w_down.astype(BF16)
    for i in range(DEPTH):
        h = _ffn_ln(h, w_up[i, 0], w_down[i, 0], ln_g[i, 0], ln_b[i, 0])
        kind, j = i % 3, i // 3
        if kind == 0:
            n_main = GDN_CONV_DIM + GDN_V_DIM
            w_in = gdn_w_in[j].astype(BF16)
            proj = _matmul(h, w_in[:, :n_main])
            tail = _matmul(h, jnp.pad(w_in[:, n_main:], ((0, 0), (0, 64))))
            proj = jnp.concatenate([proj, tail[:, :64]], axis=-1)
            o = _gdn_core_jax(proj, gdn_conv_w[j], gdn_a_log[j], gdn_dt_bias[j], gdn_norm_w[j], bsz, s)
            h = _matmul_res_ln(o, gdn_w_out[j].astype(BF16), h, ln_g[i, 1], ln_b[i, 1])
        elif kind == 1:
            n_main = 2 * GLA_K_DIM + 2 * GLA_V_DIM
            w_in = gla_w_in[j].astype(BF16)
            proj = _matmul(h, w_in[:, :n_main])
            tail = _matmul(h, jnp.pad(w_in[:, n_main:], ((0, 0), (0, 128 - GLA_GATE_RANK))))
            proj = jnp.concatenate([proj, tail[:, :GLA_GATE_RANK]], axis=-1)
            o = _gla_core_jax(proj, gla_w_gate[j], gla_gate_bias[j], gla_norm_w[j], bsz, s)
            h = _matmul_res_ln(o, gla_w_out[j].astype(BF16), h, ln_g[i, 1], ln_b[i, 1])
        else:
            y = _s5_core_jax(h, s5_a_re[j], s5_a_im[j], s5_b_re[j], s5_b_im[j], s5_c_re[j], s5_c_im[j],
                             s5_d[j], s5_log_dt[j], bsz, s)
            h = _glu_res_ln(y, s5_w_glu[j].astype(BF16), h, ln_g[i, 1], ln_b[i, 1])
        h = _ffn_ln(h, w_up[i, 1], w_down[i, 1], ln_g[i, 2], ln_b[i, 2])
    return h.reshape(bsz, s, d)
```

```python
import functools
import math

import jax
import jax.numpy as jnp
from jax import lax
from jax.experimental import pallas as pl
from jax.experimental.pallas import tpu as pltpu

F32 = jnp.float32
BF16 = jnp.bfloat16

D_MODEL = 2048
DEPTH = 4
ALPHA = (2.0 * DEPTH) ** 0.25
LN_EPS = 1e-5
RMS_EPS = 1e-6
FFN_DIM = 5632

GDN_QK_HEADS = 16
GDN_V_HEADS = 32
GDN_HEAD_DIM = 128
GDN_CONV = 4
GDN_CHUNK = 64
GDN_QK_DIM = GDN_QK_HEADS * GDN_HEAD_DIM
GDN_V_DIM = GDN_V_HEADS * GDN_HEAD_DIM
GDN_CONV_DIM = 2 * GDN_QK_DIM + GDN_V_DIM

GLA_HEADS = 4
GLA_K_DIM = D_MODEL // 2
GLA_V_DIM = D_MODEL
GLA_HEAD_K = GLA_K_DIM // GLA_HEADS
GLA_HEAD_V = GLA_V_DIM // GLA_HEADS
GLA_GATE_RANK = 16
GLA_GATE_TAU = 16.0
GLA_CHUNK = 64

S5_GROUP = 16
S5_GROUPS = D_MODEL // S5_GROUP
S5_STATE = 64
S5_CHUNK = 512

VMEM_LIMIT = 48 * 1024 * 1024


def _layer_norm(y, g, b):
    mu = jnp.mean(y, axis=-1, keepdims=True)
    yc = y - mu
    var = jnp.mean(yc * yc, axis=-1, keepdims=True)
    return yc * lax.rsqrt(var + LN_EPS) * g + b


def _ffn_kernel(x_ref, wg_ref, wu_ref, wd_ref, g_ref, b_ref, o_ref, xb_ref, acc_ref):
    f = pl.program_id(1)

    @pl.when(f == 0)
    def _():
        xb_ref[...] = x_ref[...].astype(BF16)
        acc_ref[...] = jnp.zeros_like(acc_ref)

    xb = xb_ref[...]
    gate = jnp.dot(xb, wg_ref[...], preferred_element_type=F32)
    up = jnp.dot(xb, wu_ref[...], preferred_element_type=F32)
    act = (gate * jax.nn.sigmoid(gate) * up).astype(BF16)
    acc_ref[...] += jnp.dot(act, wd_ref[...], preferred_element_type=F32)

    @pl.when(f == pl.num_programs(1) - 1)
    def _():
        y = ALPHA * x_ref[...] + 0.5 * acc_ref[...]
        o_ref[...] = _layer_norm(y, g_ref[...], b_ref[...])


def _ffn_ln(x, w_up, w_down, g, b, *, tm=512, tf=512):
    t, d = x.shape
    fdim = w_down.shape[0]
    nf = fdim // tf
    return pl.pallas_call(
        _ffn_kernel,
        out_shape=jax.ShapeDtypeStruct((t, d), F32),
        grid=(t // tm, nf),
        in_specs=[
            pl.BlockSpec((tm, d), lambda i, f: (i, 0)),
            pl.BlockSpec((d, tf), lambda i, f: (0, f)),
            pl.BlockSpec((d, tf), lambda i, f: (0, f + nf)),
            pl.BlockSpec((tf, d), lambda i, f: (f, 0)),
            pl.BlockSpec((1, d), lambda i, f: (0, 0)),
            pl.BlockSpec((1, d), lambda i, f: (0, 0)),
        ],
        out_specs=pl.BlockSpec((tm, d), lambda i, f: (i, 0)),
        scratch_shapes=[pltpu.VMEM((tm, d), BF16), pltpu.VMEM((tm, d), F32)],
        compiler_params=pltpu.CompilerParams(
            dimension_semantics=("parallel", "arbitrary"), vmem_limit_bytes=VMEM_LIMIT),
        name="ffn_ln",
    )(x, w_up, w_up, w_down, g.reshape(1, d), b.reshape(1, d))


def _mm_kernel(x_ref, w_ref, o_ref):
    o_ref[...] = jnp.dot(x_ref[...].astype(BF16), w_ref[...], preferred_element_type=F32)


def _matmul(x, w, *, tm=1024, tn=1024):
    t, k = x.shape
    n = w.shape[1]
    tn = min(tn, n)
    return pl.pallas_call(
        _mm_kernel,
        out_shape=jax.ShapeDtypeStruct((t, n), F32),
        grid=(t // tm, n // tn),
        in_specs=[
            pl.BlockSpec((tm, k), lambda i, j: (i, 0)),
            pl.BlockSpec((k, tn), lambda i, j: (0, j)),
        ],
        out_specs=pl.BlockSpec((tm, tn), lambda i, j: (i, j)),
        compiler_params=pltpu.CompilerParams(
            dimension_semantics=("parallel", "arbitrary"), vmem_limit_bytes=VMEM_LIMIT),
        name="proj",
    )(x, w)


def _mm_res_ln_kernel(a_ref, w_ref, r_ref, g_ref, b_ref, o_ref, acc_ref):
    k = pl.program_id(1)

    @pl.when(k == 0)
    def _():
        acc_ref[...] = jnp.zeros_like(acc_ref)

    acc_ref[...] += jnp.dot(a_ref[...], w_ref[...], preferred_element_type=F32)

    @pl.when(k == pl.num_programs(1) - 1)
    def _():
        y = ALPHA * r_ref[...] + acc_ref[...]
        o_ref[...] = _layer_norm(y, g_ref[...], b_ref[...])


def _matmul_res_ln(a, w, res, g, b, *, tm=512, tk=1024):
    t, kdim = a.shape
    d = w.shape[1]
    return pl.pallas_call(
        _mm_res_ln_kernel,
        out_shape=jax.ShapeDtypeStruct((t, d), F32),
        grid=(t // tm, kdim // tk),
        in_specs=[
            pl.BlockSpec((tm, tk), lambda i, k: (i, k)),
            pl.BlockSpec((tk, d), lambda i, k: (k, 0)),
            pl.BlockSpec((tm, d), lambda i, k: (i, 0)),
            pl.BlockSpec((1, d), lambda i, k: (0, 0)),
            pl.BlockSpec((1, d), lambda i, k: (0, 0)),
        ],
        out_specs=pl.BlockSpec((tm, d), lambda i, k: (i, 0)),
        scratch_shapes=[pltpu.VMEM((tm, d), F32)],
        compiler_params=pltpu.CompilerParams(
            dimension_semantics=("parallel", "arbitrary"), vmem_limit_bytes=VMEM_LIMIT),
        name="out_proj_ln",
    )(a, w, res, g.reshape(1, d), b.reshape(1, d))


def _glu_res_ln_kernel(a_ref, wv_ref, wg_ref, r_ref, g_ref, b_ref, o_ref):
    a = a_ref[...]
    val = jnp.dot(a, wv_ref[...], preferred_element_type=F32)
    gate = jnp.dot(a, wg_ref[...], preferred_element_type=F32)
    y = ALPHA * r_ref[...] + val * jax.nn.sigmoid(gate)
    o_ref[...] = _layer_norm(y, g_ref[...], b_ref[...])


def _glu_res_ln(a, w_glu, res, g, b, *, tm=256):
    t, kdim = a.shape
    d = res.shape[1]
    return pl.pallas_call(
        _glu_res_ln_kernel,
        out_shape=jax.ShapeDtypeStruct((t, d), F32),
        grid=(t // tm,),
        in_specs=[
            pl.BlockSpec((tm, kdim), lambda i: (i, 0)),
            pl.BlockSpec((kdim, d), lambda i: (0, 0)),
            pl.BlockSpec((kdim, d), lambda i: (0, 1)),
            pl.BlockSpec((tm, d), lambda i: (i, 0)),
            pl.BlockSpec((1, d), lambda i: (0, 0)),
            pl.BlockSpec((1, d), lambda i: (0, 0)),
        ],
        out_specs=pl.BlockSpec((tm, d), lambda i: (i, 0)),
        compiler_params=pltpu.CompilerParams(
            dimension_semantics=("parallel",), vmem_limit_bytes=VMEM_LIMIT),
        name="glu_ln",
    )(a, w_glu, w_glu, res, g.reshape(1, d), b.reshape(1, d))


S5_GB = 16
S5_NGB = S5_GROUPS // S5_GB
S5_SB = S5_GB * S5_STATE
S5_BATCH = 4


def _s5_discretize_kernel(are_ref, aim_ref, ldt_ref, abr_ref, abi_ref, a2r_ref, a2i_ref, cfr_ref, cfi_ref):
    lr, li = are_ref[...], aim_ref[...]
    dt = jnp.exp(ldt_ref[...])
    mag = jnp.exp(lr * dt)
    ar = mag * jnp.cos(li * dt)
    ai = mag * jnp.sin(li * dt)
    abr_ref[...] = ar
    abi_ref[...] = ai
    a2r_ref[...] = ar * ar - ai * ai
    a2i_ref[...] = 2.0 * ar * ai
    nr, ni = ar - 1.0, ai
    den = lr * lr + li * li
    cfr_ref[...] = (nr * lr + ni * li) / den
    cfi_ref[...] = (ni * lr - nr * li) / den


def _s5_scan_kernel(u_ref, bm_ref, cm_ref, d_ref, a1r_ref, a1i_ref, a0r_ref, a0i_ref, o_ref,
                    st_ref, hr_ref, hi_ref):
    tb = pl.program_id(1)
    rows = u_ref.shape[0]

    @pl.when(tb == 0)
    def _():
        hr_ref[...] = jnp.zeros_like(hr_ref)
        hi_ref[...] = jnp.zeros_like(hi_ref)

    u = u_ref[...]
    st_ref[...] = jnp.dot(u.astype(BF16), bm_ref[0], preferred_element_type=F32)

    a1r, a1i = a1r_ref[0], a1i_ref[0]
    a0r, a0i = a0r_ref[0], a0i_ref[0]
    hi_rows = lax.broadcasted_iota(jnp.int32, (8, S5_SB), 0) >= S5_BATCH

    def body(k, carry):
        pr, pi = carry
        r0 = pl.multiple_of(k * 8, 8)
        xr = st_ref[pl.ds(r0, 8), 0:S5_SB]
        xi = st_ref[pl.ds(r0, 8), S5_SB:2 * S5_SB]
        sr = pltpu.roll(xr, S5_BATCH, 0)
        si = pltpu.roll(xi, S5_BATCH, 0)
        hr = xr + (a0r * sr - a0i * si) + (a1r * pr - a1i * pi)
        hi = xi + (a0r * si + a0i * sr) + (a1r * pi + a1i * pr)
        st_ref[pl.ds(r0, 8), 0:S5_SB] = hr
        st_ref[pl.ds(r0, 8), S5_SB:2 * S5_SB] = hi
        nr = jnp.where(hi_rows, hr, pltpu.roll(hr, S5_BATCH, 0))
        ni = jnp.where(hi_rows, hi, pltpu.roll(hi, S5_BATCH, 0))
        return nr, ni

    pr, pi = lax.fori_loop(0, rows // 8, body, (hr_ref[...], hi_ref[...]))
    hr_ref[...] = pr
    hi_ref[...] = pi

    y = jnp.dot(st_ref[...].astype(BF16), cm_ref[0], preferred_element_type=F32) + d_ref[...] * u
    o_ref[...] = jax.nn.gelu(y).astype(o_ref.dtype)


def _s5_core(xt, a_re, a_im, b_re, b_im, c_re, c_im, d_skip, log_dt, *, ts=128):
    t, d = xt.shape
    rows = ts * S5_BATCH
    sds = jax.ShapeDtypeStruct((S5_GROUPS, S5_STATE), F32)
    abr, abi, a2r, a2i, cfr, cfi = pl.pallas_call(
        _s5_discretize_kernel, out_shape=(sds,) * 6, name="s5_discretize",
    )(a_re, a_im, log_dt.reshape(S5_GROUPS, 1))

    bbr = cfr[..., None] * b_re - cfi[..., None] * b_im
    bbi = cfr[..., None] * b_im + cfi[..., None] * b_re
    eye = jnp.eye(S5_GB, dtype=F32)

    def blockdiag_in(m):
        m = m.reshape(S5_NGB, S5_GB, S5_STATE, S5_GROUP)
        return jnp.einsum("ngph,gk->nghkp", m, eye).reshape(S5_NGB, S5_GB * S5_GROUP, S5_SB)

    def blockdiag_out(m):
        m = m.reshape(S5_NGB, S5_GB, S5_GROUP, S5_STATE)
        return jnp.einsum("nghp,gk->ngpkh", m, eye).reshape(S5_NGB, S5_SB, S5_GB * S5_GROUP)

    bmat = jnp.concatenate([blockdiag_in(bbr), blockdiag_in(bbi)], axis=-1).astype(BF16)
    cmat = jnp.concatenate([blockdiag_out(c_re), blockdiag_out(-c_im)], axis=1).astype(BF16)

    def tile8(top, bot):
        top = jnp.broadcast_to(top.reshape(S5_NGB, 1, S5_SB), (S5_NGB, S5_BATCH, S5_SB))
        bot = jnp.broadcast_to(bot.reshape(S5_NGB, 1, S5_SB), (S5_NGB, S5_BATCH, S5_SB))
        return jnp.concatenate([top, bot], axis=1)

    zero = jnp.zeros_like(abr)
    a1r, a1i = tile8(abr, a2r), tile8(abi, a2i)
    a0r, a0i = tile8(zero, abr), tile8(zero, abi)

    kin = S5_GB * S5_GROUP
    coef_spec = pl.BlockSpec((1, 8, S5_SB), lambda g, i: (g, 0, 0))
    return pl.pallas_call(
        _s5_scan_kernel,
        out_shape=jax.ShapeDtypeStruct((t, d), BF16),
        grid=(S5_NGB, t // rows),
        in_specs=[
            pl.BlockSpec((rows, kin), lambda g, i: (i, g)),
            pl.BlockSpec((1, kin, 2 * S5_SB), lambda g, i: (g, 0, 0)),
            pl.BlockSpec((1, 2 * S5_SB, kin), lambda g, i: (g, 0, 0)),
            pl.BlockSpec((1, kin), lambda g, i: (0, g)),
            coef_spec, coef_spec, coef_spec, coef_spec,
        ],
        out_specs=pl.BlockSpec((rows, kin), lambda g, i: (i, g)),
        scratch_shapes=[pltpu.VMEM((rows, 2 * S5_SB), F32),
                        pltpu.VMEM((8, S5_SB), F32), pltpu.VMEM((8, S5_SB), F32)],
        compiler_params=pltpu.CompilerParams(
            dimension_semantics=("parallel", "arbitrary"), vmem_limit_bytes=VMEM_LIMIT),
        name="s5_scan",
    )(xt, bmat, cmat, d_skip.reshape(1, d), a1r, a1i, a0r, a0i)


GLA_ROWS = 512
HIGHEST = lax.Precision.HIGHEST


def _log_sigmoid(z):
    return jnp.minimum(z, 0.0) - jnp.log1p(jnp.exp(-jnp.abs(z)))


def _gla_kernel(q_ref, k_ref, v_ref, r_ref, gl_ref, wg_ref, gb_ref, nw_ref, o_ref, st_ref, b_ref):
    c = GLA_CHUNK

    @pl.when(pl.program_id(2) == 0)
    def _():
        st_ref[...] = jnp.zeros_like(st_ref)

    row = lax.broadcasted_iota(jnp.int32, (c, c), 0)
    col = lax.broadcasted_iota(jnp.int32, (c, c), 1)
    tril = (row >= col).astype(F32)
    sub = lax.broadcasted_iota(jnp.int32, (8, GLA_HEAD_K), 0)
    lane = lax.broadcasted_iota(jnp.int32, (8, 128), 1)

    def chunk(ci, carry):
        r0 = pl.multiple_of(ci * c, c)
        z = jnp.dot(gl_ref[pl.ds(r0, c), :], wg_ref[...], precision=HIGHEST,
                    preferred_element_type=F32) + gb_ref[...]
        f = _log_sigmoid(z) * (1.0 / GLA_GATE_TAU)
        bc = jnp.dot(tril, f, precision=HIGHEST, preferred_element_type=F32)
        b_ref[...] = bc

        tiles = []
        for it in range(c // 8):
            qi = q_ref[pl.ds(r0 + 8 * it, 8), :] * (GLA_HEAD_K ** -0.5)
            bi = b_ref[8 * it:8 * it + 8, :]
            acc = jnp.zeros((8, 128), F32)
            for jt in range(it + 1):
                for jj in range(8):
                    j = 8 * jt + jj
                    kj = k_ref[pl.ds(r0 + j, 1), :]
                    bj = b_ref[j:j + 1, :]
                    p = qi * kj * jnp.exp(bi - bj)
                    if jt == it:
                        p = jnp.where(sub >= jj, p, 0.0)
                    s = jnp.sum(p, axis=1, keepdims=True)
                    acc = jnp.where(lane == j, s, acc)
            tiles.append(acc)
        attn = jnp.concatenate(tiles, axis=0)[:, :c]

        q = q_ref[pl.ds(r0, c), :] * (GLA_HEAD_K ** -0.5)
        k = k_ref[pl.ds(r0, c), :]
        v = v_ref[pl.ds(r0, c), :].astype(BF16)
        st = st_ref[...]
        qe = (q * jnp.exp(bc)).astype(BF16)
        o = lax.dot_general(qe, st.astype(BF16), (((1,), (1,)), ((), ())), preferred_element_type=F32)
        o = o + jnp.dot(attn.astype(BF16), v, preferred_element_type=F32)
        blast = bc[c - 1:c, :]
        ke = (k * jnp.exp(blast - bc)).astype(BF16)
        st_ref[...] = st * jnp.exp(blast) + lax.dot_general(
            v, ke, (((0,), (0,)), ((), ())), preferred_element_type=F32)

        o = o * lax.rsqrt(jnp.mean(o * o, axis=-1, keepdims=True) + RMS_EPS) * nw_ref[...]
        r = r_ref[pl.ds(r0, c), :]
        o_ref[pl.ds(r0, c), :] = (o * (r * jax.nn.sigmoid(r))).astype(o_ref.dtype)
        return carry

    lax.fori_loop(0, q_ref.shape[0] // c, chunk, 0)


def _gla_core(proj, tail, w_gate, gate_bias, norm_w, bsz, s):
    t = bsz * s
    rows = min(GLA_ROWS, s)
    nblk = s // rows
    k_blk = GLA_K_DIM // GLA_HEAD_K
    v_blk = 2 * GLA_K_DIM // GLA_HEAD_V
    r_blk = (2 * GLA_K_DIM + GLA_V_DIM) // GLA_HEAD_V
    wg = jnp.pad(w_gate, ((0, 128 - GLA_GATE_RANK), (0, 0)))
    return pl.pallas_call(
        _gla_kernel,
        out_shape=jax.ShapeDtypeStruct((t, GLA_V_DIM), BF16),
        grid=(bsz, GLA_HEADS, nblk),
        in_specs=[
            pl.BlockSpec((rows, GLA_HEAD_K), lambda b, h, i: (b * nblk + i, h)),
            pl.BlockSpec((rows, GLA_HEAD_K), lambda b, h, i: (b * nblk + i, k_blk + h)),
            pl.BlockSpec((rows, GLA_HEAD_V), lambda b, h, i: (b * nblk + i, v_blk + h)),
            pl.BlockSpec((rows, GLA_HEAD_V), lambda b, h, i: (b * nblk + i, r_blk + h)),
            pl.BlockSpec((rows, 128), lambda b, h, i: (b * nblk + i, 0)),
            pl.BlockSpec((128, GLA_HEAD_K), lambda b, h, i: (0, h)),
            pl.BlockSpec((1, GLA_HEAD_K), lambda b, h, i: (0, h)),
            pl.BlockSpec((1, GLA_HEAD_V), lambda b, h, i: (0, 0)),
        ],
        out_specs=pl.BlockSpec((rows, GLA_HEAD_V), lambda b, h, i: (b * nblk + i, h)),
        scratch_shapes=[pltpu.VMEM((GLA_HEAD_V, GLA_HEAD_K), F32),
                        pltpu.VMEM((GLA_CHUNK, GLA_HEAD_K), F32)],
        compiler_params=pltpu.CompilerParams(
            dimension_semantics=("parallel", "parallel", "arbitrary"), vmem_limit_bytes=VMEM_LIMIT),
        name="gla_chunk",
    )(proj, proj, proj, proj, tail, wg, gate_bias.reshape(1, GLA_K_DIM), norm_w.reshape(1, GLA_HEAD_V))


GDN_ROWS = 256
GDN_UNIT = 128
GDN_HALO = 8


def _softplus(x):
    return jnp.maximum(x, 0.0) + jnp.log1p(jnp.exp(-jnp.abs(x)))


def _bdot(a, b):
    return jnp.dot(a.astype(BF16), b.astype(BF16), preferred_element_type=F32)


def _gdn_kernel(alog_ref, dtb_ref, q_ref, k_ref, v_ref, z_ref, cwq_ref, cwk_ref, cwv_ref, gt_ref, nw_ref,
                o_ref, xq_ref, xk_ref, xv_ref, qn_ref, kn_ref, vc_ref, rp_ref, cp_ref,
                u_ref, w_ref, at_ref, os_ref, s_ref):
    rows, unit, c, dh = GDN_ROWS, GDN_UNIT, GDN_CHUNK, GDN_HEAD_DIM
    hk = pl.program_id(1)
    first = pl.program_id(2) == 0

    @pl.when(first)
    def _():
        s_ref[...] = jnp.zeros_like(s_ref)
        xq_ref[0:GDN_HALO, :] = jnp.zeros((GDN_HALO, dh), F32)
        xk_ref[0:GDN_HALO, :] = jnp.zeros((GDN_HALO, dh), F32)
        xv_ref[0:GDN_HALO, :] = jnp.zeros((GDN_HALO, 2 * dh), F32)

    def conv_silu(x_ref, xs_ref, cw_ref):
        xs_ref[GDN_HALO:GDN_HALO + rows, :] = x_ref[...]
        acc = None
        for j in range(GDN_CONV):
            off = GDN_HALO - (GDN_CONV - 1) + j
            term = xs_ref[off:off + rows, :] * cw_ref[j:j + 1, :]
            acc = term if acc is None else acc + term
        xs_ref[0:GDN_HALO, :] = xs_ref[rows:rows + GDN_HALO, :]
        return acc * jax.nn.sigmoid(acc)

    def l2n(x):
        return x * lax.rsqrt(jnp.sum(x * x, axis=-1, keepdims=True) + RMS_EPS)

    qn_ref[...] = l2n(conv_silu(q_ref, xq_ref, cwq_ref)) * (dh ** -0.5)
    kn_ref[...] = l2n(conv_silu(k_ref, xk_ref, cwk_ref))
    vc_ref[...] = conv_silu(v_ref, xv_ref, cwv_ref)

    rj = lax.broadcasted_iota(jnp.int32, (rows, rows), 0)
    cj = lax.broadcasted_iota(jnp.int32, (rows, rows), 1)
    same_chunk = (rj >> 6) == (cj >> 6)
    cum_mat = (same_chunk & (rj <= cj)).astype(F32)
    tot_mat = same_chunk.astype(F32)
    g_rows, beta_rows = [], []
    for e in range(2):
        h = 2 * hk + e
        b_row = gt_ref[pl.ds(h, 1), :]
        a_row = gt_ref[pl.ds(GDN_V_HEADS + h, 1), :]
        rate = jnp.exp(jnp.full((1, rows), alog_ref[h], F32))
        g_rows.append(-rate * _softplus(a_row + dtb_ref[h]))
        beta_rows.append(jax.nn.sigmoid(b_row))
    g8 = jnp.concatenate(g_rows + [jnp.zeros((6, rows), F32)], axis=0)
    gc8 = jnp.dot(g8, cum_mat, precision=HIGHEST, preferred_element_type=F32)
    gt8 = jnp.dot(g8, tot_mat, precision=HIGHEST, preferred_element_type=F32)
    rp_ref[...] = jnp.zeros_like(rp_ref)
    rp_ref[0:2, :] = gc8[0:2]
    rp_ref[2:3, :] = beta_rows[0]
    rp_ref[3:4, :] = beta_rows[1]
    rp_ref[4:6, :] = gt8[0:2]
    cp_ref[...] = rp_ref[...].T

    ru = lax.broadcasted_iota(jnp.int32, (unit, unit), 0)
    cu = lax.broadcasted_iota(jnp.int32, (unit, unit), 1)
    same = (ru >> 6) == (cu >> 6)
    causal = same & (ru >= cu)
    strict = same & (ru > cu)
    in16 = ((ru >> 4) == (cu >> 4)) & (ru > cu)
    off32 = ((ru >> 5) == (cu >> 5)) & ((ru >> 4) > (cu >> 4))
    off64 = same & ((ru >> 5) > (cu >> 5))
    eye = (ru == cu).astype(F32)

    for un in range(rows // unit):
        r0 = un * unit
        kn = kn_ref[r0:r0 + unit, :]
        knb = kn.astype(BF16)
        kk = lax.dot_general(knb, knb, (((1,), (1,)), ((), ())), preferred_element_type=F32)
        qk = lax.dot_general(qn_ref[r0:r0 + unit, :].astype(BF16), knb, (((1,), (1,)), ((), ())),
                             preferred_element_type=F32)
        for e in range(2):
            gc_col = cp_ref[r0:r0 + unit, e:e + 1]
            beta_col = cp_ref[r0:r0 + unit, 2 + e:3 + e]
            gc_row = gc8[e:e + 1, r0:r0 + unit]
            decay = jnp.where(causal, jnp.exp(gc_col - gc_row), 0.0)
            at_ref[e, r0:r0 + unit, :] = (qk * decay).astype(BF16)
            lmat = jnp.where(strict, kk * decay, 0.0) * beta_col
            y = -jnp.where(in16, lmat, 0.0)
            t = eye + y
            for _ in range(3):
                y = _bdot(y, y)
                t = t + _bdot(t, y)
            t = t - _bdot(_bdot(t, jnp.where(off32, lmat, 0.0)), t)
            t = t - _bdot(_bdot(t, jnp.where(off64, lmat, 0.0)), t)
            egc = jnp.exp(gc_col)
            rhs = jnp.concatenate([vc_ref[r0:r0 + unit, e * dh:(e + 1) * dh] * beta_col,
                                   kn * (beta_col * egc)], axis=1)
            sol = _bdot(t, rhs)
            u_ref[e, r0:r0 + unit, :] = sol[:, :dh]
            w_ref[e, r0:r0 + unit, :] = sol[:, dh:].astype(BF16)

    for ci in range(rows // c):
        r0 = ci * c
        a0 = (ci % (unit // c)) * c
        qn = qn_ref[r0:r0 + c, :]
        kn = kn_ref[r0:r0 + c, :]
        for e in range(2):
            gc_col = cp_ref[r0:r0 + c, e:e + 1]
            gt_col = cp_ref[r0:r0 + c, 4 + e:5 + e]
            s = s_ref[e]
            sb = s.astype(BF16)
            v_new = u_ref[e, r0:r0 + c, :] - jnp.dot(w_ref[e, r0:r0 + c, :], sb, preferred_element_type=F32)
            vb = v_new.astype(BF16)
            qe = (qn * jnp.exp(gc_col)).astype(BF16)
            o = jnp.dot(qe, sb, preferred_element_type=F32) + jnp.dot(
                at_ref[e, r0:r0 + c, a0:a0 + c], vb, preferred_element_type=F32)
            ke = (kn * jnp.exp(gt_col - gc_col)).astype(BF16)
            s_ref[e] = s * jnp.exp(gt_col[0:1, :]) + lax.dot_general(
                ke, vb, (((0,), (0,)), ((), ())), preferred_element_type=F32)
            os_ref[r0:r0 + c, e * dh:(e + 1) * dh] = o

    for e in range(2):
        o = os_ref[:, e * dh:(e + 1) * dh]
        o = o * lax.rsqrt(jnp.mean(o * o, axis=-1, keepdims=True) + RMS_EPS) * nw_ref[...]
        z = z_ref[:, e * dh:(e + 1) * dh]
        o_ref[:, e * dh:(e + 1) * dh] = (o * (z * jax.nn.sigmoid(z))).astype(o_ref.dtype)


def _gdn_core(proj, tail, conv_w, a_log, dt_bias, norm_w, bsz, s):
    t = bsz * s
    rows, dh = GDN_ROWS, GDN_HEAD_DIM
    nblk = s // rows
    k_blk = GDN_QK_DIM // dh
    v_blk = 2 * GDN_QK_DIM // (2 * dh)
    z_blk = GDN_CONV_DIM // (2 * dh)
    tail_t = tail.T
    smem = pl.BlockSpec(memory_space=pltpu.SMEM)
    return pl.pallas_call(
        _gdn_kernel,
        out_shape=jax.ShapeDtypeStruct((t, GDN_V_DIM), BF16),
        grid=(bsz, GDN_QK_HEADS, nblk),
        in_specs=[
            smem, smem,
            pl.BlockSpec((rows, dh), lambda b, h, i: (b * nblk + i, h)),
            pl.BlockSpec((rows, dh), lambda b, h, i: (b * nblk + i, k_blk + h)),
            pl.BlockSpec((rows, 2 * dh), lambda b, h, i: (b * nblk + i, v_blk + h)),
            pl.BlockSpec((rows, 2 * dh), lambda b, h, i: (b * nblk + i, z_blk + h)),
            pl.BlockSpec((GDN_CONV, dh), lambda b, h, i: (0, h)),
            pl.BlockSpec((GDN_CONV, dh), lambda b, h, i: (0, k_blk + h)),
            pl.BlockSpec((GDN_CONV, 2 * dh), lambda b, h, i: (0, v_blk + h)),
            pl.BlockSpec((128, rows), lambda b, h, i: (0, b * nblk + i)),
            pl.BlockSpec((1, dh), lambda b, h, i: (0, 0)),
        ],
        out_specs=pl.BlockSpec((rows, 2 * dh), lambda b, h, i: (b * nblk + i, h)),
        scratch_shapes=[
            pltpu.VMEM((rows + GDN_HALO, dh), F32),
            pltpu.VMEM((rows + GDN_HALO, dh), F32),
            pltpu.VMEM((rows + GDN_HALO, 2 * dh), F32),
            pltpu.VMEM((rows, dh), F32),
            pltpu.VMEM((rows, dh), F32),
            pltpu.VMEM((rows, 2 * dh), F32),
            pltpu.VMEM((128, rows), F32),
            pltpu.VMEM((rows, 128), F32),
            pltpu.VMEM((2, rows, dh), F32),
            pltpu.VMEM((2, rows, dh), BF16),
            pltpu.VMEM((2, rows, GDN_UNIT), BF16),
            pltpu.VMEM((rows, 2 * dh), F32),
            pltpu.VMEM((2, dh, dh), F32),
        ],
        compiler_params=pltpu.CompilerParams(
            dimension_semantics=("parallel", "parallel", "arbitrary"), vmem_limit_bytes=VMEM_LIMIT),
        name="gdn_chunk",
    )(a_log, dt_bias, proj, proj, proj, proj, conv_w, conv_w, conv_w, tail_t, norm_w.reshape(1, dh))


def _to_chunks(t, c):
    b, s = t.shape[:2]
    t = t.reshape((b, s // c, c) + t.shape[2:])
    return t.transpose((1, 0, 3, 2) + tuple(range(4, t.ndim)))


def _from_chunks(t):
    n, b, h, c = t.shape[:4]
    t = t.transpose((1, 0, 3, 2) + tuple(range(4, t.ndim)))
    return t.reshape((b, n * c, h) + t.shape[4:])


def _gdn_core_jax(proj, conv_w, a_log, dt_bias, norm_w, bsz, s):
    proj = proj.reshape(bsz, s, -1)
    qkv, z, b_in, a_in = jnp.split(
        proj, [GDN_CONV_DIM, GDN_CONV_DIM + GDN_V_DIM, GDN_CONV_DIM + GDN_V_DIM + GDN_V_HEADS], axis=-1)
    xp = jnp.pad(qkv, ((0, 0), (GDN_CONV - 1, 0), (0, 0)))
    qkv = sum(xp[:, j:j + s, :] * conv_w[j] for j in range(GDN_CONV))
    qkv = jax.nn.silu(qkv)
    q, k, v = jnp.split(qkv, [GDN_QK_DIM, 2 * GDN_QK_DIM], axis=-1)

    def l2n(x):
        return x * lax.rsqrt(jnp.sum(x * x, -1, keepdims=True) + RMS_EPS)

    q = l2n(q.reshape(bsz, s, GDN_QK_HEADS, GDN_HEAD_DIM)) * (GDN_HEAD_DIM ** -0.5)
    k = l2n(k.reshape(bsz, s, GDN_QK_HEADS, GDN_HEAD_DIM))
    v = v.reshape(bsz, s, GDN_V_HEADS, GDN_HEAD_DIM)
    beta = jax.nn.sigmoid(b_in)
    g = -jnp.exp(a_log) * jax.nn.softplus(a_in + dt_bias)
    h, dv, dk = GDN_V_HEADS, GDN_HEAD_DIM, GDN_HEAD_DIM
    rep = h // GDN_QK_HEADS
    c = GDN_CHUNK
    causal = jnp.tril(jnp.ones((c, c), bool))
    strict = jnp.tril(jnp.ones((c, c), bool), -1)
    hp = lax.Precision.HIGHEST

    def step(state, inp):
        qc, kc, vc, gc, bc = inp
        qc = jnp.repeat(qc, rep, axis=1)
        kc = jnp.repeat(kc, rep, axis=1)
        gcum = jnp.cumsum(gc, axis=-1)
        decay = jnp.exp(jnp.where(causal, gcum[..., :, None] - gcum[..., None, :], -jnp.inf))
        kb = kc * bc[..., None]
        lmat = jnp.where(strict, jnp.einsum("bhid,bhjd->bhij", kb, kc) * decay, 0.0)
        rhs = jnp.concatenate([vc * bc[..., None], kb * jnp.exp(gcum)[..., None]], axis=-1)
        sol = lax.linalg.triangular_solve(lmat, rhs, left_side=True, lower=True, unit_diagonal=True)
        u, w = sol[..., :dv], sol[..., dv:]
        v_new = u - jnp.einsum("bhck,bhkv->bhcv", w, state)
        attn = jnp.einsum("bhid,bhjd->bhij", qc, kc) * decay
        o = (jnp.einsum("bhck,bhkv->bhcv", qc * jnp.exp(gcum)[..., None], state)
             + jnp.einsum("bhij,bhjv->bhiv", attn, v_new))
        glast = gcum[..., -1]
        state = (state * jnp.exp(glast)[..., None, None]
                 + jnp.einsum("bhck,bhcv->bhkv", kc * jnp.exp(glast[..., None] - gcum)[..., None], v_new))
        return state, o

    s0 = jnp.zeros((bsz, h, dk, dv), F32)
    _, o = lax.scan(step, s0, (_to_chunks(q, c), _to_chunks(k, c), _to_chunks(v, c),
                               _to_chunks(g, c), _to_chunks(beta, c)))
    o = _from_chunks(o)
    o = o * lax.rsqrt(jnp.mean(o * o, -1, keepdims=True) + RMS_EPS) * norm_w
    o = o * jax.nn.silu(z.reshape(bsz, s, GDN_V_HEADS, GDN_HEAD_DIM))
    return o.reshape(bsz * s, GDN_V_DIM).astype(BF16)


def _gla_core_jax(proj, w_gate, gate_bias, norm_w, bsz, s):
    proj = proj.reshape(bsz, s, -1)
    q, k, v, r, gl = jnp.split(
        proj, [GLA_K_DIM, 2 * GLA_K_DIM, 2 * GLA_K_DIM + GLA_V_DIM, 2 * GLA_K_DIM + 2 * GLA_V_DIM], axis=-1)
    log_f = jax.nn.log_sigmoid(gl @ w_gate + gate_bias) / GLA_GATE_TAU
    q = q.reshape(bsz, s, GLA_HEADS, GLA_HEAD_K) * (GLA_HEAD_K ** -0.5)
    k = k.reshape(bsz, s, GLA_HEADS, GLA_HEAD_K)
    v = v.reshape(bsz, s, GLA_HEADS, GLA_HEAD_V)
    log_f = log_f.reshape(bsz, s, GLA_HEADS, GLA_HEAD_K)
    c = GLA_CHUNK
    causal = jnp.tril(jnp.ones((c, c), bool))[..., None]

    def step(state, inp):
        qc, kc, vc, fc = inp
        bcum = jnp.cumsum(fc, axis=2)
        decay = jnp.exp(jnp.where(causal, bcum[:, :, :, None, :] - bcum[:, :, None, :, :], -jnp.inf))
        attn = jnp.einsum("bhik,bhjk,bhijk->bhij", qc, kc, decay)
        o = (jnp.einsum("bhck,bhkv->bhcv", qc * jnp.exp(bcum), state)
             + jnp.einsum("bhij,bhjv->bhiv", attn, vc))
        blast = bcum[:, :, -1]
        state = (state * jnp.exp(blast)[..., None]
                 + jnp.einsum("bhck,bhcv->bhkv", kc * jnp.exp(blast[:, :, None] - bcum), vc))
        return state, o

    s0 = jnp.zeros((bsz, GLA_HEADS, GLA_HEAD_K, GLA_HEAD_V), F32)
    _, o = lax.scan(step, s0, (_to_chunks(q, c), _to_chunks(k, c), _to_chunks(v, c), _to_chunks(log_f, c)))
    o = _from_chunks(o)
    o = o * lax.rsqrt(jnp.mean(o * o, -1, keepdims=True) + RMS_EPS) * norm_w
    o = o * jax.nn.silu(r.reshape(bsz, s, GLA_HEADS, GLA_HEAD_V))
    return o.reshape(bsz * s, GLA_V_DIM).astype(BF16)


def _s5_core_jax(x, a_re, a_im, b_re, b_im, c_re, c_im, d_skip, log_dt, bsz, s):
    x = x.reshape(bsz, s, D_MODEL)
    lc = math.gcd(s, S5_CHUNK)
    u = x.reshape(bsz, s // lc, lc, S5_GROUPS, S5_GROUP).transpose(1, 2, 0, 3, 4)
    lam = lax.complex(a_re, a_im)
    dt = jnp.exp(log_dt)[:, None]
    a_bar = jnp.exp(lam * dt)
    b_bar = ((a_bar - 1.0) / lam)[..., None] * lax.complex(b_re, b_im)
    c_mat = lax.complex(c_re, c_im)
    d = d_skip.reshape(S5_GROUPS, S5_GROUP)

    def op(e1, e2):
        a1, b1 = e1
        a2, b2 = e2
        return a1 * a2, a2 * b1 + b2

    def step(h, uc):
        bu = jnp.einsum("tbgh,gph->tbgp", uc.astype(jnp.complex64), b_bar)
        a = jnp.broadcast_to(a_bar, bu.shape)
        a_cum, b_cum = lax.associative_scan(op, (a, bu), axis=0)
        states = b_cum + a_cum * h[None]
        y = jnp.real(jnp.einsum("tbgp,ghp->tbgh", states, c_mat)) + d * uc
        return states[-1], y

    h0 = jnp.zeros((bsz, S5_GROUPS, S5_STATE), jnp.complex64)
    _, y = lax.scan(step, h0, u)
    y = jax.nn.gelu(y.transpose(2, 0, 1, 3, 4).reshape(bsz * s, D_MODEL))
    return y.astype(BF16)


def kernel(x, ln_g, ln_b, ffn_w_up, ffn_w_down, gdn_w_in, gdn_conv_w, gdn_a_log, gdn_dt_bias,
           gdn_norm_w, gdn_w_out, gla_w_in, gla_w_gate, gla_gate_bias, gla_norm_w, gla_w_out,
           s5_a_re, s5_a_im, s5_b_re, s5_b_im, s5_c_re, s5_c_im, s5_d, s5_log_dt, s5_w_glu):
    bsz, s, d = x.shape
    h = x.reshape(bsz * s, d)
    w_up = ffn_w_up.astype(BF16)
    w_down = ffn_w_down.astype(BF16)
    for i in range(DEPTH):
        h = _ffn_ln(h, w_up[i, 0], w_down[i, 0], ln_g[i, 0], ln_b[i, 0])
        kind, j = i % 3, i // 3
        if kind == 0:
            n_main = GDN_CONV_DIM + GDN_V_DIM
            w_in = gdn_w_in[j].astype(BF16)
            proj = _matmul(h, w_in[:, :n_main])
            tail = _matmul(h, jnp.pad(w_in[:, n_main:], ((0, 0), (0, 128 - 2 * GDN_V_HEADS))))
            o = _gdn_core(proj, tail, gdn_conv_w[j], gdn_a_log[j], gdn_dt_bias[j], gdn_norm_w[j], bsz, s)
            h = _matmul_res_ln(o, gdn_w_out[j].astype(BF16), h, ln_g[i, 1], ln_b[i, 1])
        elif kind == 1:
            n_main = 2 * GLA_K_DIM + 2 * GLA_V_DIM
            w_in = gla_w_in[j].astype(BF16)
            proj = _matmul(h, w_in[:, :n_main])
            tail = _matmul(h, jnp.pad(w_in[:, n_main:], ((0, 0), (0, 128 - GLA_GATE_RANK))))
            o = _gla_core(proj, tail, gla_w_gate[j], gla_gate_bias[j], gla_norm_w[j], bsz, s)
            h = _matmul_res_ln(o, gla_w_out[j].astype(BF16), h, ln_g[i, 1], ln_b[i, 1])
        else:
            ht = h.reshape(bsz, s, d).transpose(1, 0, 2).reshape(s * bsz, d)
            y = _s5_core(ht, s5_a_re[j], s5_a_im[j], s5_b_re[j], s5_b_im[j], s5_c_re[j], s5_c_im[j],
                         s5_d[j], s5_log_dt[j])
            ht = _glu_res_ln(y, s5_w_glu[j].astype(BF16), ht, ln_g[i, 1], ln_b[i, 1])
            h = ht.reshape(s, bsz, d).transpose(1, 0, 2).reshape(bsz * s, d)
        h = _ffn_ln(h, w_up[i, 1], w_down[i, 1], ln_g[i, 2], ln_b[i, 2])
    return h.reshape(bsz, s, d)
```

```python
import jax
import jax.numpy as jnp
from jax import lax
from jax.experimental import pallas as pl
from jax.experimental.pallas import tpu as pltpu

F32 = jnp.float32
BF16 = jnp.bfloat16
HIGHEST = lax.Precision.HIGHEST

D_MODEL = 2048
DEPTH = 4
ALPHA = (2.0 * DEPTH) ** 0.25
LN_EPS = 1e-5
RMS_EPS = 1e-6

GDN_QK_HEADS = 16
GDN_V_HEADS = 32
GDN_HEAD_DIM = 128
GDN_CONV = 4
GDN_CHUNK = 64
GDN_QK_DIM = GDN_QK_HEADS * GDN_HEAD_DIM
GDN_V_DIM = GDN_V_HEADS * GDN_HEAD_DIM
GDN_CONV_DIM = 2 * GDN_QK_DIM + GDN_V_DIM

GLA_HEADS = 4
GLA_K_DIM = D_MODEL // 2
GLA_V_DIM = D_MODEL
GLA_HEAD_K = GLA_K_DIM // GLA_HEADS
GLA_HEAD_V = GLA_V_DIM // GLA_HEADS
GLA_GATE_RANK = 16
GLA_GATE_TAU = 16.0
GLA_CHUNK = 64

S5_GROUP = 16
S5_GROUPS = D_MODEL // S5_GROUP
S5_STATE = 64

VMEM_LIMIT = 48 * 1024 * 1024
VMEM_LIMIT_FFN = 56 * 1024 * 1024


def _layer_norm(y, g, b):
    mu = jnp.mean(y, axis=-1, keepdims=True)
    yc = y - mu
    var = jnp.mean(yc * yc, axis=-1, keepdims=True)
    return yc * lax.rsqrt(var + LN_EPS) * g + b


def _softplus(x):
    return jnp.maximum(x, 0.0) + jnp.log1p(jnp.exp(-jnp.abs(x)))


def _log_sigmoid(z):
    return jnp.minimum(z, 0.0) - jnp.log1p(jnp.exp(-jnp.abs(z)))


def _silu(x):
    return x * jax.nn.sigmoid(x)


def _dot(a, b):
    return jnp.dot(a, b, preferred_element_type=F32)


def _dot_tn(a, b):
    return lax.dot_general(a, b, (((0,), (0,)), ((), ())), preferred_element_type=F32)


def _dot_nt(a, b):
    return lax.dot_general(a, b, (((1,), (1,)), ((), ())), preferred_element_type=F32)


def _ffn_kernel(x_ref, wg_ref, wu_ref, wd_ref, g_ref, b_ref, o_ref, xb_ref, acc_ref):
    f = pl.program_id(1)

    @pl.when(f == 0)
    def _():
        xb_ref[...] = x_ref[...].astype(BF16)
        acc_ref[...] = jnp.zeros_like(acc_ref)

    xb = xb_ref[...]
    gate = _dot(xb, wg_ref[...])
    up = _dot(xb, wu_ref[...])
    act = (_silu(gate) * up).astype(BF16)
    acc_ref[...] += _dot(act, wd_ref[...])

    @pl.when(f == pl.num_programs(1) - 1)
    def _():
        y = ALPHA * x_ref[...] + 0.5 * acc_ref[...]
        o_ref[...] = _layer_norm(y, g_ref[...], b_ref[...])


def _ffn_ln(x, w_up, w_down, g, b, *, tm=1024, tf=512):
    t, d = x.shape
    fdim = w_down.shape[0]
    nf = fdim // tf
    once = pl.Buffered(1)
    return pl.pallas_call(
        _ffn_kernel,
        out_shape=jax.ShapeDtypeStruct((t, d), F32),
        grid=(t // tm, nf),
        in_specs=[
            pl.BlockSpec((tm, d), lambda i, f: (i, 0), pipeline_mode=once),
            pl.BlockSpec((d, tf), lambda i, f: (0, f)),
            pl.BlockSpec((d, tf), lambda i, f: (0, f + nf)),
            pl.BlockSpec((tf, d), lambda i, f: (f, 0)),
            pl.BlockSpec((1, d), lambda i, f: (0, 0)),
            pl.BlockSpec((1, d), lambda i, f: (0, 0)),
        ],
        out_specs=pl.BlockSpec((tm, d), lambda i, f: (i, 0), pipeline_mode=once),
        scratch_shapes=[pltpu.VMEM((tm, d), BF16), pltpu.VMEM((tm, d), F32)],
        compiler_params=pltpu.CompilerParams(
            dimension_semantics=("parallel", "arbitrary"), vmem_limit_bytes=VMEM_LIMIT_FFN),
        name="ffn_ln",
    )(x, w_up, w_up, w_down, g.reshape(1, d), b.reshape(1, d))


def _mm_kernel(x_ref, w_ref, o_ref):
    o_ref[...] = _dot(x_ref[...].astype(BF16), w_ref[...])


def _matmul(x, w, *, tm=1024, tn=1024):
    t, k = x.shape
    n = w.shape[1]
    tn = min(tn, n)
    return pl.pallas_call(
        _mm_kernel,
        out_shape=jax.ShapeDtypeStruct((t, n), F32),
        grid=(t // tm, n // tn),
        in_specs=[
            pl.BlockSpec((tm, k), lambda i, j: (i, 0)),
            pl.BlockSpec((k, tn), lambda i, j: (0, j)),
        ],
        out_specs=pl.BlockSpec((tm, tn), lambda i, j: (i, j)),
        compiler_params=pltpu.CompilerParams(
            dimension_semantics=("parallel", "arbitrary"), vmem_limit_bytes=VMEM_LIMIT),
        name="proj",
    )(x, w)


def _mm_res_ln_kernel(a_ref, w_ref, r_ref, g_ref, b_ref, o_ref, acc_ref):
    k = pl.program_id(1)

    @pl.when(k == 0)
    def _():
        acc_ref[...] = jnp.zeros_like(acc_ref)

    acc_ref[...] += _dot(a_ref[...], w_ref[...])

    @pl.when(k == pl.num_programs(1) - 1)
    def _():
        y = ALPHA * r_ref[...] + acc_ref[...]
        o_ref[...] = _layer_norm(y, g_ref[...], b_ref[...])


def _matmul_res_ln(a, w, res, g, b, *, tm=512, tk=1024):
    t, kdim = a.shape
    d = w.shape[1]
    return pl.pallas_call(
        _mm_res_ln_kernel,
        out_shape=jax.ShapeDtypeStruct((t, d), F32),
        grid=(t // tm, kdim // tk),
        in_specs=[
            pl.BlockSpec((tm, tk), lambda i, k: (i, k)),
            pl.BlockSpec((tk, d), lambda i, k: (k, 0)),
            pl.BlockSpec((tm, d), lambda i, k: (i, 0)),
            pl.BlockSpec((1, d), lambda i, k: (0, 0)),
            pl.BlockSpec((1, d), lambda i, k: (0, 0)),
        ],
        out_specs=pl.BlockSpec((tm, d), lambda i, k: (i, 0)),
        scratch_shapes=[pltpu.VMEM((tm, d), F32)],
        compiler_params=pltpu.CompilerParams(
            dimension_semantics=("parallel", "arbitrary"), vmem_limit_bytes=VMEM_LIMIT),
        name="out_proj_ln",
    )(a, w, res, g.reshape(1, d), b.reshape(1, d))


def _glu_res_ln_kernel(a_ref, wv_ref, wg_ref, r_ref, g_ref, b_ref, o_ref):
    a = a_ref[...]
    val = _dot(a, wv_ref[...])
    gate = _dot(a, wg_ref[...])
    y = ALPHA * r_ref[...] + val * jax.nn.sigmoid(gate)
    o_ref[...] = _layer_norm(y, g_ref[...], b_ref[...])


def _glu_res_ln(a, w_glu, res, g, b, *, tm=256):
    t, kdim = a.shape
    d = res.shape[1]
    return pl.pallas_call(
        _glu_res_ln_kernel,
        out_shape=jax.ShapeDtypeStruct((t, d), F32),
        grid=(t // tm,),
        in_specs=[
            pl.BlockSpec((tm, kdim), lambda i: (i, 0)),
            pl.BlockSpec((kdim, d), lambda i: (0, 0)),
            pl.BlockSpec((kdim, d), lambda i: (0, 1)),
            pl.BlockSpec((tm, d), lambda i: (i, 0)),
            pl.BlockSpec((1, d), lambda i: (0, 0)),
            pl.BlockSpec((1, d), lambda i: (0, 0)),
        ],
        out_specs=pl.BlockSpec((tm, d), lambda i: (i, 0)),
        compiler_params=pltpu.CompilerParams(
            dimension_semantics=("parallel",), vmem_limit_bytes=VMEM_LIMIT),
        name="glu_ln",
    )(a, w_glu, w_glu, res, g.reshape(1, d), b.reshape(1, d))


S5_GB = 16
S5_NGB = S5_GROUPS // S5_GB
S5_SB = S5_GB * S5_STATE
S5_BATCH = 4


def _s5_discretize_kernel(are_ref, aim_ref, ldt_ref, abr_ref, abi_ref, a2r_ref, a2i_ref, cfr_ref, cfi_ref):
    lr, li = are_ref[...], aim_ref[...]
    dt = jnp.exp(ldt_ref[...])
    mag = jnp.exp(lr * dt)
    ar = mag * jnp.cos(li * dt)
    ai = mag * jnp.sin(li * dt)
    abr_ref[...] = ar
    abi_ref[...] = ai
    a2r_ref[...] = ar * ar - ai * ai
    a2i_ref[...] = 2.0 * ar * ai
    nr, ni = ar - 1.0, ai
    den = lr * lr + li * li
    cfr_ref[...] = (nr * lr + ni * li) / den
    cfi_ref[...] = (ni * lr - nr * li) / den


def _s5_scan_kernel(u_ref, bm_ref, cm_ref, d_ref, a1r_ref, a1i_ref, a0r_ref, a0i_ref, o_ref,
                    st_ref, hr_ref, hi_ref):
    tb = pl.program_id(1)
    rows = u_ref.shape[0]

    @pl.when(tb == 0)
    def _():
        hr_ref[...] = jnp.zeros_like(hr_ref)
        hi_ref[...] = jnp.zeros_like(hi_ref)

    u = u_ref[...]
    st_ref[...] = _dot(u.astype(BF16), bm_ref[0])

    a1r, a1i = a1r_ref[0], a1i_ref[0]
    a0r, a0i = a0r_ref[0], a0i_ref[0]
    hi_rows = lax.broadcasted_iota(jnp.int32, (8, S5_SB), 0) >= S5_BATCH

    def body(k, carry):
        pr, pi = carry
        r0 = pl.multiple_of(k * 8, 8)
        xr = st_ref[pl.ds(r0, 8), 0:S5_SB]
        xi = st_ref[pl.ds(r0, 8), S5_SB:2 * S5_SB]
        sr = pltpu.roll(xr, S5_BATCH, 0)
        si = pltpu.roll(xi, S5_BATCH, 0)
        hr = xr + (a0r * sr - a0i * si) + (a1r * pr - a1i * pi)
        hi = xi + (a0r * si + a0i * sr) + (a1r * pi + a1i * pr)
        st_ref[pl.ds(r0, 8), 0:S5_SB] = hr
        st_ref[pl.ds(r0, 8), S5_SB:2 * S5_SB] = hi
        nr = jnp.where(hi_rows, hr, pltpu.roll(hr, S5_BATCH, 0))
        ni = jnp.where(hi_rows, hi, pltpu.roll(hi, S5_BATCH, 0))
        return nr, ni

    pr, pi = lax.fori_loop(0, rows // 8, body, (hr_ref[...], hi_ref[...]))
    hr_ref[...] = pr
    hi_ref[...] = pi

    y = _dot(st_ref[...].astype(BF16), cm_ref[0]) + d_ref[...] * u
    o_ref[...] = jax.nn.gelu(y).astype(o_ref.dtype)


def _s5_core(xt, a_re, a_im, b_re, b_im, c_re, c_im, d_skip, log_dt, *, ts=128):
    t, d = xt.shape
    rows = ts * S5_BATCH
    sds = jax.ShapeDtypeStruct((S5_GROUPS, S5_STATE), F32)
    abr, abi, a2r, a2i, cfr, cfi = pl.pallas_call(
        _s5_discretize_kernel, out_shape=(sds,) * 6, name="s5_discretize",
    )(a_re, a_im, log_dt.reshape(S5_GROUPS, 1))

    bbr = cfr[..., None] * b_re - cfi[..., None] * b_im
    bbi = cfr[..., None] * b_im + cfi[..., None] * b_re
    eye = jnp.eye(S5_GB, dtype=F32)

    def blockdiag_in(m):
        m = m.reshape(S5_NGB, S5_GB, S5_STATE, S5_GROUP)
        return jnp.einsum("ngph,gk->nghkp", m, eye).reshape(S5_NGB, S5_GB * S5_GROUP, S5_SB)

    def blockdiag_out(m):
        m = m.reshape(S5_NGB, S5_GB, S5_GROUP, S5_STATE)
        return jnp.einsum("nghp,gk->ngpkh", m, eye).reshape(S5_NGB, S5_SB, S5_GB * S5_GROUP)

    bmat = jnp.concatenate([blockdiag_in(bbr), blockdiag_in(bbi)], axis=-1).astype(BF16)
    cmat = jnp.concatenate([blockdiag_out(c_re), blockdiag_out(-c_im)], axis=1).astype(BF16)

    def tile8(top, bot):
        top = jnp.broadcast_to(top.reshape(S5_NGB, 1, S5_SB), (S5_NGB, S5_BATCH, S5_SB))
        bot = jnp.broadcast_to(bot.reshape(S5_NGB, 1, S5_SB), (S5_NGB, S5_BATCH, S5_SB))
        return jnp.concatenate([top, bot], axis=1)

    zero = jnp.zeros_like(abr)
    a1r, a1i = tile8(abr, a2r), tile8(abi, a2i)
    a0r, a0i = tile8(zero, abr), tile8(zero, abi)

    kin = S5_GB * S5_GROUP
    coef_spec = pl.BlockSpec((1, 8, S5_SB), lambda g, i: (g, 0, 0))
    return pl.pallas_call(
        _s5_scan_kernel,
        out_shape=jax.ShapeDtypeStruct((t, d), BF16),
        grid=(S5_NGB, t // rows),
        in_specs=[
            pl.BlockSpec((rows, kin), lambda g, i: (i, g)),
            pl.BlockSpec((1, kin, 2 * S5_SB), lambda g, i: (g, 0, 0)),
            pl.BlockSpec((1, 2 * S5_SB, kin), lambda g, i: (g, 0, 0)),
            pl.BlockSpec((1, kin), lambda g, i: (0, g)),
            coef_spec, coef_spec, coef_spec, coef_spec,
        ],
        out_specs=pl.BlockSpec((rows, kin), lambda g, i: (i, g)),
        scratch_shapes=[pltpu.VMEM((rows, 2 * S5_SB), F32),
                        pltpu.VMEM((8, S5_SB), F32), pltpu.VMEM((8, S5_SB), F32)],
        compiler_params=pltpu.CompilerParams(
            dimension_semantics=("parallel", "arbitrary"), vmem_limit_bytes=VMEM_LIMIT),
        name="s5_scan",
    )(xt, bmat, cmat, d_skip.reshape(1, d), a1r, a1i, a0r, a0i)


GLA_ROWS = 512


def _gla_kernel(q_ref, k_ref, v_ref, r_ref, gl_ref, wg_ref, gb_ref, nw_ref, o_ref, st_ref, b_ref):
    c = GLA_CHUNK

    @pl.when(pl.program_id(2) == 0)
    def _():
        st_ref[...] = jnp.zeros_like(st_ref)

    row = lax.broadcasted_iota(jnp.int32, (c, c), 0)
    col = lax.broadcasted_iota(jnp.int32, (c, c), 1)
    tril = (row >= col).astype(F32)
    sub = lax.broadcasted_iota(jnp.int32, (8, GLA_HEAD_K), 0)
    lane = lax.broadcasted_iota(jnp.int32, (8, 128), 1)

    def chunk(ci, carry):
        r0 = pl.multiple_of(ci * c, c)
        z = jnp.dot(gl_ref[pl.ds(r0, c), :], wg_ref[...], precision=HIGHEST,
                    preferred_element_type=F32) + gb_ref[...]
        f = _log_sigmoid(z) * (1.0 / GLA_GATE_TAU)
        bc = jnp.dot(tril, f, precision=HIGHEST, preferred_element_type=F32)
        b_ref[...] = bc

        tiles = []
        for it in range(c // 8):
            qi = q_ref[pl.ds(r0 + 8 * it, 8), :] * (GLA_HEAD_K ** -0.5)
            bi = b_ref[8 * it:8 * it + 8, :]
            acc = jnp.zeros((8, 128), F32)
            for jt in range(it + 1):
                for jj in range(8):
                    j = 8 * jt + jj
                    kj = k_ref[pl.ds(r0 + j, 1), :]
                    bj = b_ref[j:j + 1, :]
                    p = qi * kj * jnp.exp(bi - bj)
                    if jt == it:
                        p = jnp.where(sub >= jj, p, 0.0)
                    s = jnp.sum(p, axis=1, keepdims=True)
                    acc = jnp.where(lane == j, s, acc)
            tiles.append(acc)
        attn = jnp.concatenate(tiles, axis=0)[:, :c]

        q = q_ref[pl.ds(r0, c), :] * (GLA_HEAD_K ** -0.5)
        k = k_ref[pl.ds(r0, c), :]
        v = v_ref[pl.ds(r0, c), :].astype(BF16)
        st = st_ref[...]
        qe = (q * jnp.exp(bc)).astype(BF16)
        o = _dot_nt(qe, st.astype(BF16)) + _dot(attn.astype(BF16), v)
        blast = bc[c - 1:c, :]
        ke = (k * jnp.exp(blast - bc)).astype(BF16)
        st_ref[...] = st * jnp.exp(blast) + _dot_tn(v, ke)

        o = o * lax.rsqrt(jnp.mean(o * o, axis=-1, keepdims=True) + RMS_EPS) * nw_ref[...]
        o_ref[pl.ds(r0, c), :] = (o * _silu(r_ref[pl.ds(r0, c), :])).astype(o_ref.dtype)
        return carry

    lax.fori_loop(0, q_ref.shape[0] // c, chunk, 0)


def _gla_core(proj, tail, w_gate, gate_bias, norm_w, bsz, s):
    t = bsz * s
    rows = min(GLA_ROWS, s)
    nblk = s // rows
    k_blk = GLA_K_DIM // GLA_HEAD_K
    v_blk = 2 * GLA_K_DIM // GLA_HEAD_V
    r_blk = (2 * GLA_K_DIM + GLA_V_DIM) // GLA_HEAD_V
    wg = jnp.pad(w_gate, ((0, 128 - GLA_GATE_RANK), (0, 0)))
    return pl.pallas_call(
        _gla_kernel,
        out_shape=jax.ShapeDtypeStruct((t, GLA_V_DIM), BF16),
        grid=(bsz, GLA_HEADS, nblk),
        in_specs=[
            pl.BlockSpec((rows, GLA_HEAD_K), lambda b, h, i: (b * nblk + i, h)),
            pl.BlockSpec((rows, GLA_HEAD_K), lambda b, h, i: (b * nblk + i, k_blk + h)),
            pl.BlockSpec((rows, GLA_HEAD_V), lambda b, h, i: (b * nblk + i, v_blk + h)),
            pl.BlockSpec((rows, GLA_HEAD_V), lambda b, h, i: (b * nblk + i, r_blk + h)),
            pl.BlockSpec((rows, 128), lambda b, h, i: (b * nblk + i, 0)),
            pl.BlockSpec((128, GLA_HEAD_K), lambda b, h, i: (0, h)),
            pl.BlockSpec((1, GLA_HEAD_K), lambda b, h, i: (0, h)),
            pl.BlockSpec((1, GLA_HEAD_V), lambda b, h, i: (0, 0)),
        ],
        out_specs=pl.BlockSpec((rows, GLA_HEAD_V), lambda b, h, i: (b * nblk + i, h)),
        scratch_shapes=[pltpu.VMEM((GLA_HEAD_V, GLA_HEAD_K), F32),
                        pltpu.VMEM((GLA_CHUNK, GLA_HEAD_K), F32)],
        compiler_params=pltpu.CompilerParams(
            dimension_semantics=("parallel", "parallel", "arbitrary"), vmem_limit_bytes=VMEM_LIMIT),
        name="gla_chunk",
    )(proj, proj, proj, proj, tail, wg, gate_bias.reshape(1, GLA_K_DIM), norm_w.reshape(1, GLA_HEAD_V))


GDN_ROWS = 256
GDN_QH = 2
GDN_VH = 2 * GDN_QH
GDN_HALO = 8


def _gdn_kernel(alog_ref, dtb_ref, q_ref, k_ref, v_ref, z_ref, cwq_ref, cwk_ref, cwv_ref, gt_ref, nw_ref,
                o_ref, xq_ref, xk_ref, xv_ref, rp_ref, cp_ref, s_ref):
    rows, c, dh, nv = GDN_ROWS, GDN_CHUNK, GDN_HEAD_DIM, GDN_VH
    nchunk = rows // c
    hp = pl.program_id(1)

    @pl.when(pl.program_id(2) == 0)
    def _():
        s_ref[...] = jnp.zeros_like(s_ref)
        xq_ref[0:GDN_HALO, :] = jnp.zeros((GDN_HALO, xq_ref.shape[1]), F32)
        xk_ref[0:GDN_HALO, :] = jnp.zeros((GDN_HALO, xk_ref.shape[1]), F32)
        xv_ref[0:GDN_HALO, :] = jnp.zeros((GDN_HALO, xv_ref.shape[1]), F32)

    def conv_silu(x_ref, xs_ref, cw_ref):
        xs_ref[GDN_HALO:GDN_HALO + rows, :] = x_ref[...]
        acc = None
        for j in range(GDN_CONV):
            off = GDN_HALO - (GDN_CONV - 1) + j
            term = xs_ref[off:off + rows, :] * cw_ref[j:j + 1, :]
            acc = term if acc is None else acc + term
        xs_ref[0:GDN_HALO, :] = xs_ref[rows:rows + GDN_HALO, :]
        return _silu(acc)

    def l2n(x):
        return x * lax.rsqrt(jnp.sum(x * x, axis=-1, keepdims=True) + RMS_EPS)

    qc = conv_silu(q_ref, xq_ref, cwq_ref)
    kc = conv_silu(k_ref, xk_ref, cwk_ref)
    vc = conv_silu(v_ref, xv_ref, cwv_ref)
    qn = [l2n(qc[:, i * dh:(i + 1) * dh]) * (dh ** -0.5) for i in range(GDN_QH)]
    kn = [l2n(kc[:, i * dh:(i + 1) * dh]) for i in range(GDN_QH)]

    ri = lax.broadcasted_iota(jnp.int32, (rows, rows), 0)
    ci = lax.broadcasted_iota(jnp.int32, (rows, rows), 1)
    same = (ri >> 6) == (ci >> 6)
    g_rows, beta_rows = [], []
    for e in range(nv):
        h = nv * hp + e
        b_row = gt_ref[pl.ds(h, 1), :]
        a_row = gt_ref[pl.ds(GDN_V_HEADS + h, 1), :]
        rate = jnp.exp(jnp.full((1, rows), alog_ref[h], F32))
        g_rows.append(-rate * _softplus(a_row + dtb_ref[h]))
        beta_rows.append(jax.nn.sigmoid(b_row))
    g8 = jnp.concatenate(g_rows + [jnp.zeros((8 - nv, rows), F32)], axis=0)
    gc8 = jnp.dot(g8, (same & (ri <= ci)).astype(F32), precision=HIGHEST, preferred_element_type=F32)
    gt8 = jnp.dot(g8, same.astype(F32), precision=HIGHEST, preferred_element_type=F32)
    rp_ref[...] = jnp.zeros_like(rp_ref)
    rp_ref[0:8, :] = gc8
    for e in range(nv):
        rp_ref[8 + e:9 + e, :] = beta_rows[e]
    rp_ref[16:24, :] = gt8
    cp_ref[...] = rp_ref[...].T

    causal = same & (ri >= ci)
    strict = same & (ri > ci)
    in16 = ((ri >> 4) == (ci >> 4)) & (ri > ci)
    off32 = ((ri >> 5) == (ci >> 5)) & ((ri >> 4) > (ci >> 4))
    off64 = same & ((ri >> 5) > (ci >> 5))
    eye = (ri == ci).astype(F32)

    knb = [x.astype(BF16) for x in kn]
    kk = [_dot_nt(x, x) for x in knb]
    qk = [_dot_nt(qn[i].astype(BF16), knb[i]) for i in range(GDN_QH)]

    heads = range(nv)
    gc_col = [cp_ref[:, e:e + 1] for e in heads]
    beta_col = [cp_ref[:, 8 + e:9 + e] for e in heads]
    gt_col = [cp_ref[:, 16 + e:17 + e] for e in heads]
    decay = [jnp.where(causal, jnp.exp(gc_col[e] - gc8[e:e + 1, :]), 0.0) for e in heads]
    attn = [(qk[e // 2] * decay[e]).astype(BF16) for e in heads]
    lmat = [jnp.where(strict, kk[e // 2] * decay[e], 0.0) * beta_col[e] for e in heads]
    rhs = [jnp.concatenate([vc[:, e * dh:(e + 1) * dh] * beta_col[e],
                            kn[e // 2] * (beta_col[e] * jnp.exp(gc_col[e]))], axis=1).astype(BF16)
           for e in heads]

    y32 = [-jnp.where(in16, lm, 0.0) for lm in lmat]
    ts = [eye + y for y in y32]
    ys = [y.astype(BF16) for y in y32]
    for _ in range(3):
        ys = [_dot(y, y).astype(BF16) for y in ys]
        ts = [t + _dot(t.astype(BF16), y) for t, y in zip(ts, ys)]
    for off in (off32, off64):
        lo = [jnp.where(off, lm, 0.0).astype(BF16) for lm in lmat]
        tb = [t.astype(BF16) for t in ts]
        tl = [_dot(t, l).astype(BF16) for t, l in zip(tb, lo)]
        ts = [t - _dot(a, b) for t, a, b in zip(ts, tl, tb)]
    sol = [_dot(t.astype(BF16), r).astype(BF16) for t, r in zip(ts, rhs)]

    awu = [_dot(a, s) for a, s in zip(attn, sol)]
    qeff = [(qn[e // 2] * jnp.exp(gc_col[e]) - awu[e][:, dh:]).astype(BF16) for e in heads]
    kt = [(kn[e // 2] * jnp.exp(gt_col[e] - gc_col[e])).astype(BF16) for e in heads]
    kwu = [[_dot_tn(kt[e][j * c:(j + 1) * c, :], sol[e][j * c:(j + 1) * c, :]) for e in heads]
           for j in range(nchunk)]

    state = [s_ref[e] for e in heads]
    outs = [[] for _ in heads]
    for j in range(nchunk):
        sb = [s.astype(BF16) for s in state]
        for e in heads:
            outs[e].append(_dot(qeff[e][j * c:(j + 1) * c, :], sb[e]) + awu[e][j * c:(j + 1) * c, :dh])
        state = [state[e] * jnp.exp(gt_col[e][j * c:j * c + 1, :]) + kwu[j][e][:, :dh]
                 - _dot(kwu[j][e][:, dh:].astype(BF16), sb[e]) for e in heads]
    for e in heads:
        s_ref[e] = state[e]
        o = jnp.concatenate(outs[e], axis=0)
        o = o * lax.rsqrt(jnp.mean(o * o, axis=-1, keepdims=True) + RMS_EPS) * nw_ref[...]
        o_ref[:, e * dh:(e + 1) * dh] = (o * _silu(z_ref[:, e * dh:(e + 1) * dh])).astype(o_ref.dtype)


def _gdn_core(proj, tail, conv_w, a_log, dt_bias, norm_w, bsz, s):
    t = bsz * s
    rows, dh = GDN_ROWS, GDN_HEAD_DIM
    qw, vw = GDN_QH * dh, GDN_VH * dh
    nblk = s // rows
    k_blk = GDN_QK_DIM // qw
    v_blk = 2 * GDN_QK_DIM // vw
    z_blk = GDN_CONV_DIM // vw
    tail_t = tail.T
    smem = pl.BlockSpec(memory_space=pltpu.SMEM)
    return pl.pallas_call(
        _gdn_kernel,
        out_shape=jax.ShapeDtypeStruct((t, GDN_V_DIM), BF16),
        grid=(bsz, GDN_QK_HEADS // GDN_QH, nblk),
        in_specs=[
            smem, smem,
            pl.BlockSpec((rows, qw), lambda b, h, i: (b * nblk + i, h)),
            pl.BlockSpec((rows, qw), lambda b, h, i: (b * nblk + i, k_blk + h)),
            pl.BlockSpec((rows, vw), lambda b, h, i: (b * nblk + i, v_blk + h)),
            pl.BlockSpec((rows, vw), lambda b, h, i: (b * nblk + i, z_blk + h)),
            pl.BlockSpec((GDN_CONV, qw), lambda b, h, i: (0, h)),
            pl.BlockSpec((GDN_CONV, qw), lambda b, h, i: (0, k_blk + h)),
            pl.BlockSpec((GDN_CONV, vw), lambda b, h, i: (0, v_blk + h)),
            pl.BlockSpec((128, rows), lambda b, h, i: (0, b * nblk + i)),
            pl.BlockSpec((1, dh), lambda b, h, i: (0, 0)),
        ],
        out_specs=pl.BlockSpec((rows, vw), lambda b, h, i: (b * nblk + i, h)),
        scratch_shapes=[
            pltpu.VMEM((rows + GDN_HALO, qw), F32),
            pltpu.VMEM((rows + GDN_HALO, qw), F32),
            pltpu.VMEM((rows + GDN_HALO, vw), F32),
            pltpu.VMEM((128, rows), F32),
            pltpu.VMEM((rows, 128), F32),
            pltpu.VMEM((GDN_VH, dh, dh), F32),
        ],
        compiler_params=pltpu.CompilerParams(
            dimension_semantics=("parallel", "parallel", "arbitrary"), vmem_limit_bytes=VMEM_LIMIT),
        name="gdn_chunk",
    )(a_log, dt_bias, proj, proj, proj, proj, conv_w, conv_w, conv_w, tail_t, norm_w.reshape(1, dh))


def kernel(x, ln_g, ln_b, ffn_w_up, ffn_w_down, gdn_w_in, gdn_conv_w, gdn_a_log, gdn_dt_bias,
           gdn_norm_w, gdn_w_out, gla_w_in, gla_w_gate, gla_gate_bias, gla_norm_w, gla_w_out,
           s5_a_re, s5_a_im, s5_b_re, s5_b_im, s5_c_re, s5_c_im, s5_d, s5_log_dt, s5_w_glu):
    bsz, s, d = x.shape
    h = x.reshape(bsz * s, d)
    w_up = ffn_w_up.astype(BF16)
    w_down = ffn_w_down.astype(BF16)
    for i in range(DEPTH):
        h = _ffn_ln(h, w_up[i, 0], w_down[i, 0], ln_g[i, 0], ln_b[i, 0])
        kind, j = i % 3, i // 3
        if kind == 0:
            n_main = GDN_CONV_DIM + GDN_V_DIM
            w_in = gdn_w_in[j].astype(BF16)
            proj = _matmul(h, w_in[:, :n_main])
            tail = _matmul(h, jnp.pad(w_in[:, n_main:], ((0, 0), (0, 128 - 2 * GDN_V_HEADS))))
            o = _gdn_core(proj, tail, gdn_conv_w[j], gdn_a_log[j], gdn_dt_bias[j], gdn_norm_w[j], bsz, s)
            h = _matmul_res_ln(o, gdn_w_out[j].astype(BF16), h, ln_g[i, 1], ln_b[i, 1])
        elif kind == 1:
            n_main = 2 * GLA_K_DIM + 2 * GLA_V_DIM
            w_in = gla_w_in[j].astype(BF16)
            proj = _matmul(h, w_in[:, :n_main])
            tail = _matmul(h, jnp.pad(w_in[:, n_main:], ((0, 0), (0, 128 - GLA_GATE_RANK))))
            o = _gla_core(proj, tail, gla_w_gate[j], gla_gate_bias[j], gla_norm_w[j], bsz, s)
            h = _matmul_res_ln(o, gla_w_out[j].astype(BF16), h, ln_g[i, 1], ln_b[i, 1])
        else:
            ht = h.reshape(bsz, s, d).transpose(1, 0, 2).reshape(s * bsz, d)
            y = _s5_core(ht, s5_a_re[j], s5_a_im[j], s5_b_re[j], s5_b_im[j], s5_c_re[j], s5_c_im[j],
                         s5_d[j], s5_log_dt[j])
            ht = _glu_res_ln(y, s5_w_glu[j].astype(BF16), ht, ln_g[i, 1], ln_b[i, 1])
            h = ht.reshape(s, bsz, d).transpose(1, 0, 2).reshape(bsz * s, d)
        h = _ffn_ln(h, w_up[i, 1], w_down[i, 1], ln_g[i, 2], ln_b[i, 2])
    return h.reshape(bsz, s, d)
```

```python
import jax
import jax.numpy as jnp
from jax import lax
from jax.experimental import pallas as pl
from jax.experimental.pallas import tpu as pltpu

F32 = jnp.float32
BF16 = jnp.bfloat16

D_MODEL = 2048
DEPTH = 4
ALPHA = (2.0 * DEPTH) ** 0.25
LN_EPS = 1e-5
RMS_EPS = 1e-6

GDN_QK_HEADS = 16
GDN_V_HEADS = 32
GDN_HEAD_DIM = 128
GDN_CONV = 4
GDN_CHUNK = 64
GDN_QK_DIM = GDN_QK_HEADS * GDN_HEAD_DIM
GDN_V_DIM = GDN_V_HEADS * GDN_HEAD_DIM
GDN_CONV_DIM = 2 * GDN_QK_DIM + GDN_V_DIM

GLA_HEADS = 4
GLA_K_DIM = D_MODEL // 2
GLA_V_DIM = D_MODEL
GLA_HEAD_K = GLA_K_DIM // GLA_HEADS
GLA_HEAD_V = GLA_V_DIM // GLA_HEADS
GLA_GATE_RANK = 16
GLA_GATE_TAU = 16.0
GLA_CHUNK = 64

S5_GROUP = 16
S5_GROUPS = D_MODEL // S5_GROUP
S5_STATE = 64

VMEM_LIMIT = 48 * 1024 * 1024


def _layer_norm(y, g, b):
    mu = jnp.mean(y, axis=-1, keepdims=True)
    yc = y - mu
    var = jnp.mean(yc * yc, axis=-1, keepdims=True)
    return yc * lax.rsqrt(var + LN_EPS) * g + b


def _softplus(x):
    return jnp.maximum(x, 0.0) + jnp.log1p(jnp.exp(-jnp.abs(x)))


def _log_sigmoid(z):
    return jnp.minimum(z, 0.0) - jnp.log1p(jnp.exp(-jnp.abs(z)))


def _silu(x):
    return x * jax.nn.sigmoid(x)


def _dot(a, b):
    return jnp.dot(a, b, preferred_element_type=F32)


def _dot_tn(a, b):
    return lax.dot_general(a, b, (((0,), (0,)), ((), ())), preferred_element_type=F32)


def _dot_nt(a, b):
    return lax.dot_general(a, b, (((1,), (1,)), ((), ())), preferred_element_type=F32)


def _split3(x):
    hi = x.astype(BF16)
    r1 = x - hi.astype(F32)
    mid = r1.astype(BF16)
    lo = (r1 - mid.astype(F32)).astype(BF16)
    return hi, mid, lo


def _ffn_kernel(x_ref, wg_ref, wu_ref, wd_ref, g_ref, b_ref, o_ref, xb_ref, acc_ref):
    f = pl.program_id(1)

    @pl.when(f == 0)
    def _():
        xb_ref[...] = x_ref[...].astype(BF16)
        acc_ref[...] = jnp.zeros_like(acc_ref)

    xb = xb_ref[...]
    gate = _dot(xb, wg_ref[...])
    up = _dot(xb, wu_ref[...])
    act = (_silu(gate) * up).astype(BF16)
    acc_ref[...] += _dot(act, wd_ref[...])

    @pl.when(f == pl.num_programs(1) - 1)
    def _():
        y = ALPHA * x_ref[...] + 0.5 * acc_ref[...]
        o_ref[...] = _layer_norm(y, g_ref[...], b_ref[...])


def _ffn_ln(x, w_up, w_down, g, b, *, tm=512, tf=512):
    t, d = x.shape
    fdim = w_down.shape[0]
    nf = fdim // tf
    return pl.pallas_call(
        _ffn_kernel,
        out_shape=jax.ShapeDtypeStruct((t, d), F32),
        grid=(t // tm, nf),
        in_specs=[
            pl.BlockSpec((tm, d), lambda i, f: (i, 0)),
            pl.BlockSpec((d, tf), lambda i, f: (0, f)),
            pl.BlockSpec((d, tf), lambda i, f: (0, f + nf)),
            pl.BlockSpec((tf, d), lambda i, f: (f, 0)),
            pl.BlockSpec((1, d), lambda i, f: (0, 0)),
            pl.BlockSpec((1, d), lambda i, f: (0, 0)),
        ],
        out_specs=pl.BlockSpec((tm, d), lambda i, f: (i, 0)),
        scratch_shapes=[pltpu.VMEM((tm, d), BF16), pltpu.VMEM((tm, d), F32)],
        compiler_params=pltpu.CompilerParams(
            dimension_semantics=("parallel", "arbitrary"), vmem_limit_bytes=VMEM_LIMIT),
        name="ffn_ln",
    )(x, w_up, w_up, w_down, g.reshape(1, d), b.reshape(1, d))


def _mm_kernel(x_ref, w_ref, o_ref):
    o_ref[...] = _dot(x_ref[...].astype(BF16), w_ref[...])


def _matmul(x, w, *, tm=1024, tn=1024):
    t, k = x.shape
    n = w.shape[1]
    tn = min(tn, n)
    return pl.pallas_call(
        _mm_kernel,
        out_shape=jax.ShapeDtypeStruct((t, n), F32),
        grid=(t // tm, n // tn),
        in_specs=[
            pl.BlockSpec((tm, k), lambda i, j: (i, 0)),
            pl.BlockSpec((k, tn), lambda i, j: (0, j)),
        ],
        out_specs=pl.BlockSpec((tm, tn), lambda i, j: (i, j)),
        compiler_params=pltpu.CompilerParams(
            dimension_semantics=("parallel", "arbitrary"), vmem_limit_bytes=VMEM_LIMIT),
        name="proj",
    )(x, w)


def _mm_res_ln_kernel(a_ref, w_ref, r_ref, g_ref, b_ref, o_ref, acc_ref):
    k = pl.program_id(1)

    @pl.when(k == 0)
    def _():
        acc_ref[...] = jnp.zeros_like(acc_ref)

    acc_ref[...] += _dot(a_ref[...], w_ref[...])

    @pl.when(k == pl.num_programs(1) - 1)
    def _():
        y = ALPHA * r_ref[...] + acc_ref[...]
        o_ref[...] = _layer_norm(y, g_ref[...], b_ref[...])


def _matmul_res_ln(a, w, res, g, b, *, tm=512, tk=1024):
    t, kdim = a.shape
    d = w.shape[1]
    return pl.pallas_call(
        _mm_res_ln_kernel,
        out_shape=jax.ShapeDtypeStruct((t, d), F32),
        grid=(t // tm, kdim // tk),
        in_specs=[
            pl.BlockSpec((tm, tk), lambda i, k: (i, k)),
            pl.BlockSpec((tk, d), lambda i, k: (k, 0)),
            pl.BlockSpec((tm, d), lambda i, k: (i, 0)),
            pl.BlockSpec((1, d), lambda i, k: (0, 0)),
            pl.BlockSpec((1, d), lambda i, k: (0, 0)),
        ],
        out_specs=pl.BlockSpec((tm, d), lambda i, k: (i, 0)),
        scratch_shapes=[pltpu.VMEM((tm, d), F32)],
        compiler_params=pltpu.CompilerParams(
            dimension_semantics=("parallel", "arbitrary"), vmem_limit_bytes=VMEM_LIMIT),
        name="out_proj_ln",
    )(a, w, res, g.reshape(1, d), b.reshape(1, d))


def _glu_res_ln_kernel(a_ref, wv_ref, wg_ref, r_ref, g_ref, b_ref, o_ref):
    a = a_ref[...]
    val = _dot(a, wv_ref[...])
    gate = _dot(a, wg_ref[...])
    y = ALPHA * r_ref[...] + val * jax.nn.sigmoid(gate)
    o_ref[...] = _layer_norm(y, g_ref[...], b_ref[...])


def _glu_res_ln(a, w_glu, res, g, b, *, tm=256):
    t, kdim = a.shape
    d = res.shape[1]
    return pl.pallas_call(
        _glu_res_ln_kernel,
        out_shape=jax.ShapeDtypeStruct((t, d), F32),
        grid=(t // tm,),
        in_specs=[
            pl.BlockSpec((tm, kdim), lambda i: (i, 0)),
            pl.BlockSpec((kdim, d), lambda i: (0, 0)),
            pl.BlockSpec((kdim, d), lambda i: (0, 1)),
            pl.BlockSpec((tm, d), lambda i: (i, 0)),
            pl.BlockSpec((1, d), lambda i: (0, 0)),
            pl.BlockSpec((1, d), lambda i: (0, 0)),
        ],
        out_specs=pl.BlockSpec((tm, d), lambda i: (i, 0)),
        compiler_params=pltpu.CompilerParams(
            dimension_semantics=("parallel",), vmem_limit_bytes=VMEM_LIMIT),
        name="glu_ln",
    )(a, w_glu, w_glu, res, g.reshape(1, d), b.reshape(1, d))


S5_GB = 16
S5_NGB = S5_GROUPS // S5_GB
S5_SB = S5_GB * S5_STATE
S5_BATCH = 4


def _s5_discretize_kernel(are_ref, aim_ref, ldt_ref, abr_ref, abi_ref, a2r_ref, a2i_ref, cfr_ref, cfi_ref):
    lr, li = are_ref[...], aim_ref[...]
    dt = jnp.exp(ldt_ref[...])
    mag = jnp.exp(lr * dt)
    ar = mag * jnp.cos(li * dt)
    ai = mag * jnp.sin(li * dt)
    abr_ref[...] = ar
    abi_ref[...] = ai
    a2r_ref[...] = ar * ar - ai * ai
    a2i_ref[...] = 2.0 * ar * ai
    nr, ni = ar - 1.0, ai
    den = lr * lr + li * li
    cfr_ref[...] = (nr * lr + ni * li) / den
    cfi_ref[...] = (ni * lr - nr * li) / den


def _s5_scan_kernel(u_ref, bm_ref, cm_ref, d_ref, a1r_ref, a1i_ref, a0r_ref, a0i_ref, o_ref,
                    st_ref, hr_ref, hi_ref):
    tb = pl.program_id(1)
    rows = u_ref.shape[0]

    @pl.when(tb == 0)
    def _():
        hr_ref[...] = jnp.zeros_like(hr_ref)
        hi_ref[...] = jnp.zeros_like(hi_ref)

    u = u_ref[...]
    st_ref[...] = _dot(u.astype(BF16), bm_ref[0])

    a1r, a1i = a1r_ref[0], a1i_ref[0]
    a0r, a0i = a0r_ref[0], a0i_ref[0]
    hi_rows = lax.broadcasted_iota(jnp.int32, (8, S5_SB), 0) >= S5_BATCH

    def body(k, carry):
        pr, pi = carry
        r0 = pl.multiple_of(k * 8, 8)
        xr = st_ref[pl.ds(r0, 8), 0:S5_SB]
        xi = st_ref[pl.ds(r0, 8), S5_SB:2 * S5_SB]
        sr = pltpu.roll(xr, S5_BATCH, 0)
        si = pltpu.roll(xi, S5_BATCH, 0)
        hr = xr + (a0r * sr - a0i * si) + (a1r * pr - a1i * pi)
        hi = xi + (a0r * si + a0i * sr) + (a1r * pi + a1i * pr)
        st_ref[pl.ds(r0, 8), 0:S5_SB] = hr
        st_ref[pl.ds(r0, 8), S5_SB:2 * S5_SB] = hi
        nr = jnp.where(hi_rows, hr, pltpu.roll(hr, S5_BATCH, 0))
        ni = jnp.where(hi_rows, hi, pltpu.roll(hi, S5_BATCH, 0))
        return nr, ni

    pr, pi = lax.fori_loop(0, rows // 8, body, (hr_ref[...], hi_ref[...]))
    hr_ref[...] = pr
    hi_ref[...] = pi

    y = _dot(st_ref[...].astype(BF16), cm_ref[0]) + d_ref[...] * u
    o_ref[...] = jax.nn.gelu(y).astype(o_ref.dtype)


def _s5_core(xt, a_re, a_im, b_re, b_im, c_re, c_im, d_skip, log_dt, *, ts=128):
    t, d = xt.shape
    rows = ts * S5_BATCH
    sds = jax.ShapeDtypeStruct((S5_GROUPS, S5_STATE), F32)
    abr, abi, a2r, a2i, cfr, cfi = pl.pallas_call(
        _s5_discretize_kernel, out_shape=(sds,) * 6, name="s5_discretize",
    )(a_re, a_im, log_dt.reshape(S5_GROUPS, 1))

    bbr = cfr[..., None] * b_re - cfi[..., None] * b_im
    bbi = cfr[..., None] * b_im + cfi[..., None] * b_re
    eye = jnp.eye(S5_GB, dtype=F32)

    def blockdiag_in(m):
        m = m.reshape(S5_NGB, S5_GB, S5_STATE, S5_GROUP)
        return jnp.einsum("ngph,gk->nghkp", m, eye).reshape(S5_NGB, S5_GB * S5_GROUP, S5_SB)

    def blockdiag_out(m):
        m = m.reshape(S5_NGB, S5_GB, S5_GROUP, S5_STATE)
        return jnp.einsum("nghp,gk->ngpkh", m, eye).reshape(S5_NGB, S5_SB, S5_GB * S5_GROUP)

    bmat = jnp.concatenate([blockdiag_in(bbr), blockdiag_in(bbi)], axis=-1).astype(BF16)
    cmat = jnp.concatenate([blockdiag_out(c_re), blockdiag_out(-c_im)], axis=1).astype(BF16)

    def tile8(top, bot):
        top = jnp.broadcast_to(top.reshape(S5_NGB, 1, S5_SB), (S5_NGB, S5_BATCH, S5_SB))
        bot = jnp.broadcast_to(bot.reshape(S5_NGB, 1, S5_SB), (S5_NGB, S5_BATCH, S5_SB))
        return jnp.concatenate([top, bot], axis=1)

    zero = jnp.zeros_like(abr)
    a1r, a1i = tile8(abr, a2r), tile8(abi, a2i)
    a0r, a0i = tile8(zero, abr), tile8(zero, abi)

    kin = S5_GB * S5_GROUP
    coef_spec = pl.BlockSpec((1, 8, S5_SB), lambda g, i: (g, 0, 0))
    return pl.pallas_call(
        _s5_scan_kernel,
        out_shape=jax.ShapeDtypeStruct((t, d), BF16),
        grid=(S5_NGB, t // rows),
        in_specs=[
            pl.BlockSpec((rows, kin), lambda g, i: (i, g)),
            pl.BlockSpec((1, kin, 2 * S5_SB), lambda g, i: (g, 0, 0)),
            pl.BlockSpec((1, 2 * S5_SB, kin), lambda g, i: (g, 0, 0)),
            pl.BlockSpec((1, kin), lambda g, i: (0, g)),
            coef_spec, coef_spec, coef_spec, coef_spec,
        ],
        out_specs=pl.BlockSpec((rows, kin), lambda g, i: (i, g)),
        scratch_shapes=[pltpu.VMEM((rows, 2 * S5_SB), F32),
                        pltpu.VMEM((8, S5_SB), F32), pltpu.VMEM((8, S5_SB), F32)],
        compiler_params=pltpu.CompilerParams(
            dimension_semantics=("parallel", "arbitrary"), vmem_limit_bytes=VMEM_LIMIT),
        name="s5_scan",
    )(xt, bmat, cmat, d_skip.reshape(1, d), a1r, a1i, a0r, a0i)


GLA_ROWS = 512

GLA_SUB = 16


def _gla_kernel(q_ref, k_ref, v_ref, r_ref, gl_ref, wg_ref, gb_ref, nw_ref, o_ref, st_ref, f_ref, b_ref):
    c, sc, dk, dv = GLA_CHUNK, GLA_SUB, GLA_HEAD_K, GLA_HEAD_V
    heads = range(GLA_HEADS)

    @pl.when(pl.program_id(1) == 0)
    def _():
        st_ref[...] = jnp.zeros_like(st_ref)

    g_hi, g_mid, _ = _split3(gl_ref[...])
    w_hi, w_mid, _ = _split3(wg_ref[...])
    z = _dot(g_hi, w_hi) + (_dot(g_hi, w_mid) + _dot(g_mid, w_hi)) + gb_ref[...]
    f_ref[...] = _log_sigmoid(z) * (1.0 / GLA_GATE_TAU)

    row = lax.broadcasted_iota(jnp.int32, (c, c), 0)
    col = lax.broadcasted_iota(jnp.int32, (c, c), 1)
    tril = (row >= col).astype(BF16)
    sub = lax.broadcasted_iota(jnp.int32, (8, GLA_HEAD_K), 0)
    lane = lax.broadcasted_iota(jnp.int32, (8, 128), 1)

    def chunk(ci, carry):
        r0 = pl.multiple_of(ci * c, c)
        bc = sum(_dot(tril, part) for part in _split3(f_ref[pl.ds(r0, c), :]))
        b_ref[...] = bc

        tiles = [[] for _ in heads]
        for si in range(c // sc):
            base = si * sc
            qs = [q_ref[pl.ds(r0 + base, sc), h * dk:(h + 1) * dk] * (dk ** -0.5) for h in heads]
            bs = [b_ref[base:base + sc, h * dk:(h + 1) * dk] for h in heads]
            if si == 0:
                off = [jnp.zeros((sc, 128), F32) for _ in heads]
            else:
                off = []
                for h in heads:
                    ref_b = b_ref[base - 1:base, h * dk:(h + 1) * dk]
                    qh = (qs[h] * jnp.exp(bs[h] - ref_b)).astype(BF16)
                    kh = (k_ref[pl.ds(r0, base), h * dk:(h + 1) * dk]
                          * jnp.exp(ref_b - b_ref[0:base, h * dk:(h + 1) * dk])).astype(BF16)
                    kh = jnp.concatenate([kh, jnp.zeros((128 - base, dk), BF16)], axis=0)
                    off.append(_dot_nt(qh, kh))
            for h in heads:
                for t2 in range(sc // 8):
                    qi = qs[h][8 * t2:8 * t2 + 8, :]
                    bi = bs[h][8 * t2:8 * t2 + 8, :]
                    acc = off[h][8 * t2:8 * t2 + 8, :]
                    for jj in range(8 * (t2 + 1)):
                        j = base + jj
                        kj = k_ref[pl.ds(r0 + j, 1), h * dk:(h + 1) * dk]
                        bj = b_ref[j:j + 1, h * dk:(h + 1) * dk]
                        p = qi * kj * jnp.exp(bi - bj)
                        if jj >= 8 * t2:
                            p = jnp.where(sub >= jj - 8 * t2, p, 0.0)
                        s = jnp.sum(p, axis=1, keepdims=True)
                        acc = jnp.where(lane == j, s, acc)
                    tiles[h].append(acc)
        attn = [jnp.concatenate(tiles[h], axis=0)[:, :c].astype(BF16) for h in heads]

        blast = bc[c - 1:c, :]
        qe = (q_ref[pl.ds(r0, c), :] * (dk ** -0.5) * jnp.exp(bc)).astype(BF16)
        ke = (k_ref[pl.ds(r0, c), :] * jnp.exp(blast - bc)).astype(BF16)
        eb = jnp.exp(blast)
        st = [st_ref[h] for h in heads]
        v = [v_ref[pl.ds(r0, c), h * dv:(h + 1) * dv].astype(BF16) for h in heads]
        o = [_dot_nt(qe[:, h * dk:(h + 1) * dk], st[h].astype(BF16)) + _dot(attn[h], v[h]) for h in heads]
        for h in heads:
            st_ref[h] = st[h] * eb[:, h * dk:(h + 1) * dk] + _dot_tn(v[h], ke[:, h * dk:(h + 1) * dk])
        for h in heads:
            oh = o[h] * lax.rsqrt(jnp.mean(o[h] * o[h], axis=-1, keepdims=True) + RMS_EPS) * nw_ref[...]
            gate = _silu(r_ref[pl.ds(r0, c), h * dv:(h + 1) * dv])
            o_ref[pl.ds(r0, c), h * dv:(h + 1) * dv] = (oh * gate).astype(o_ref.dtype)
        return carry

    lax.fori_loop(0, q_ref.shape[0] // c, chunk, 0)


def _gla_core(proj, tail, w_gate, gate_bias, norm_w, bsz, s):
    t = bsz * s
    rows = min(GLA_ROWS, s)
    nblk = s // rows
    wg = jnp.pad(w_gate, ((0, 128 - GLA_GATE_RANK), (0, 0)))
    return pl.pallas_call(
        _gla_kernel,
        out_shape=jax.ShapeDtypeStruct((t, GLA_V_DIM), BF16),
        grid=(bsz, nblk),
        in_specs=[
            pl.BlockSpec((rows, GLA_K_DIM), lambda b, i: (b * nblk + i, 0)),
            pl.BlockSpec((rows, GLA_K_DIM), lambda b, i: (b * nblk + i, 1)),
            pl.BlockSpec((rows, GLA_V_DIM), lambda b, i: (b * nblk + i, 1)),
            pl.BlockSpec((rows, GLA_V_DIM), lambda b, i: (b * nblk + i, 2)),
            pl.BlockSpec((rows, 128), lambda b, i: (b * nblk + i, 0)),
            pl.BlockSpec((128, GLA_K_DIM), lambda b, i: (0, 0)),
            pl.BlockSpec((1, GLA_K_DIM), lambda b, i: (0, 0)),
            pl.BlockSpec((1, GLA_HEAD_V), lambda b, i: (0, 0)),
        ],
        out_specs=pl.BlockSpec((rows, GLA_V_DIM), lambda b, i: (b * nblk + i, 0)),
        scratch_shapes=[pltpu.VMEM((GLA_HEADS, GLA_HEAD_V, GLA_HEAD_K), F32),
                        pltpu.VMEM((rows, GLA_K_DIM), F32),
                        pltpu.VMEM((GLA_CHUNK, GLA_K_DIM), F32)],
        compiler_params=pltpu.CompilerParams(
            dimension_semantics=("parallel", "arbitrary"), vmem_limit_bytes=VMEM_LIMIT),
        name="gla_chunk",
    )(proj, proj, proj, proj, tail, wg, gate_bias.reshape(1, GLA_K_DIM), norm_w.reshape(1, GLA_HEAD_V))


GDN_ROWS = 256
GDN_QH = 2
GDN_VH = 2 * GDN_QH
GDN_HALO = 8


def _gdn_kernel(alog_ref, dtb_ref, q_ref, k_ref, v_ref, z_ref, cwq_ref, cwk_ref, cwv_ref, gt_ref, nw_ref,
                o_ref, xq_ref, xk_ref, xv_ref, rp_ref, cp_ref, s_ref):
    rows, c, dh, nv = GDN_ROWS, GDN_CHUNK, GDN_HEAD_DIM, GDN_VH
    nchunk = rows // c
    hp = pl.program_id(1)

    @pl.when(pl.program_id(2) == 0)
    def _():
        s_ref[...] = jnp.zeros_like(s_ref)
        xq_ref[...] = jnp.zeros_like(xq_ref)
        xk_ref[...] = jnp.zeros_like(xk_ref)
        xv_ref[...] = jnp.zeros_like(xv_ref)

    def conv_silu(x_ref, xs_ref, cw_ref):
        width = x_ref.shape[1]
        cur = x_ref[...].reshape(rows // 8, 8, width)
        prev = jnp.concatenate([xs_ref[...].reshape(1, 8, width), cur[:-1]], axis=0)
        sub = lax.broadcasted_iota(jnp.int32, (rows // 8, 8, width), 1)
        acc = cur * cw_ref[GDN_CONV - 1:GDN_CONV, :]
        for s in range(1, GDN_CONV):
            shifted = pltpu.roll(jnp.where(sub >= 8 - s, prev, cur), s, 1)
            acc = acc + shifted * cw_ref[GDN_CONV - 1 - s:GDN_CONV - s, :]
        xs_ref[...] = x_ref[rows - GDN_HALO:rows, :]
        return _silu(acc).reshape(rows, width)

    def l2n(x):
        return x * lax.rsqrt(jnp.sum(x * x, axis=-1, keepdims=True) + RMS_EPS)

    qc = conv_silu(q_ref, xq_ref, cwq_ref)
    kc = conv_silu(k_ref, xk_ref, cwk_ref)
    vc = conv_silu(v_ref, xv_ref, cwv_ref)
    qn = [l2n(qc[:, i * dh:(i + 1) * dh]) * (dh ** -0.5) for i in range(GDN_QH)]
    kn = [l2n(kc[:, i * dh:(i + 1) * dh]) for i in range(GDN_QH)]

    ri = lax.broadcasted_iota(jnp.int32, (rows, rows), 0)
    ci = lax.broadcasted_iota(jnp.int32, (rows, rows), 1)
    same = (ri >> 6) == (ci >> 6)
    g_rows, beta_rows = [], []
    for e in range(nv):
        h = nv * hp + e
        b_row = gt_ref[pl.ds(h, 1), :]
        a_row = gt_ref[pl.ds(GDN_V_HEADS + h, 1), :]
        rate = jnp.exp(jnp.full((1, rows), alog_ref[h], F32))
        g_rows.append(-rate * _softplus(a_row + dtb_ref[h]))
        beta_rows.append(jax.nn.sigmoid(b_row))
    g8 = jnp.concatenate(g_rows + [jnp.zeros((8 - nv, rows), F32)], axis=0)
    sum_mats = jnp.concatenate([(same & (ri <= ci)).astype(BF16), same.astype(BF16)], axis=1)
    sums = sum(_dot(part, sum_mats) for part in _split3(g8))
    gc8, gt8 = sums[:, :rows], sums[:, rows:]
    rp_ref[...] = jnp.zeros_like(rp_ref)
    rp_ref[0:8, :] = gc8
    for e in range(nv):
        rp_ref[8 + e:9 + e, :] = beta_rows[e]
    rp_ref[16:24, :] = gt8
    cp_ref[...] = rp_ref[...].T

    causal = same & (ri >= ci)
    strict = same & (ri > ci)
    in16 = ((ri >> 4) == (ci >> 4)) & (ri > ci)
    off32 = ((ri >> 5) == (ci >> 5)) & ((ri >> 4) > (ci >> 4))
    off64 = same & ((ri >> 5) > (ci >> 5))
    eye = (ri == ci).astype(F32)

    knb = [x.astype(BF16) for x in kn]
    kk = [_dot_nt(x, x) for x in knb]
    qk = [_dot_nt(qn[i].astype(BF16), knb[i]) for i in range(GDN_QH)]

    heads = range(nv)
    gc_col = [cp_ref[:, e:e + 1] for e in heads]
    beta_col = [cp_ref[:, 8 + e:9 + e] for e in heads]
    gt_col = [cp_ref[:, 16 + e:17 + e] for e in heads]
    decay = [jnp.where(causal, jnp.exp(gc_col[e] - gc8[e:e + 1, :]), 0.0) for e in heads]
    attn = [(qk[e // 2] * decay[e]).astype(BF16) for e in heads]
    lmat = [jnp.where(strict, kk[e // 2] * decay[e], 0.0) * beta_col[e] for e in heads]
    rhs = [jnp.concatenate([vc[:, e * dh:(e + 1) * dh] * beta_col[e],
                            kn[e // 2] * (beta_col[e] * jnp.exp(gc_col[e]))], axis=1).astype(BF16)
           for e in heads]

    y32 = [-jnp.where(in16, lm, 0.0) for lm in lmat]
    ts = [eye + y for y in y32]
    ys = [y.astype(BF16) for y in y32]
    for _ in range(3):
        ys = [_dot(y, y).astype(BF16) for y in ys]
        ts = [t + _dot(t.astype(BF16), y) for t, y in zip(ts, ys)]
    for off in (off32, off64):
        lo = [jnp.where(off, lm, 0.0).astype(BF16) for lm in lmat]
        tb = [t.astype(BF16) for t in ts]
        tl = [_dot(t, l).astype(BF16) for t, l in zip(tb, lo)]
        ts = [t - _dot(a, b) for t, a, b in zip(ts, tl, tb)]
    sol = [_dot(t.astype(BF16), r).astype(BF16) for t, r in zip(ts, rhs)]

    awu = [_dot(a, s) for a, s in zip(attn, sol)]
    qeff = [(qn[e // 2] * jnp.exp(gc_col[e]) - awu[e][:, dh:]).astype(BF16) for e in heads]
    kt = [(kn[e // 2] * jnp.exp(gt_col[e] - gc_col[e])).astype(BF16) for e in heads]
    kwu = [[_dot_tn(kt[e][j * c:(j + 1) * c, :], sol[e][j * c:(j + 1) * c, :]) for e in heads]
           for j in range(nchunk)]

    state = [s_ref[e] for e in heads]
    outs = [[] for _ in heads]
    for j in range(nchunk):
        sb = [s.astype(BF16) for s in state]
        for e in heads:
            outs[e].append(_dot(qeff[e][j * c:(j + 1) * c, :], sb[e]) + awu[e][j * c:(j + 1) * c, :dh])
        state = [state[e] * jnp.exp(gt_col[e][j * c:j * c + 1, :]) + kwu[j][e][:, :dh]
                 - _dot(kwu[j][e][:, dh:].astype(BF16), sb[e]) for e in heads]
    for e in heads:
        s_ref[e] = state[e]
        o = jnp.concatenate(outs[e], axis=0)
        o = o * lax.rsqrt(jnp.mean(o * o, axis=-1, keepdims=True) + RMS_EPS) * nw_ref[...]
        o_ref[:, e * dh:(e + 1) * dh] = (o * _silu(z_ref[:, e * dh:(e + 1) * dh])).astype(o_ref.dtype)


def _gdn_core(proj, tail, conv_w, a_log, dt_bias, norm_w, bsz, s):
    t = bsz * s
    rows, dh = GDN_ROWS, GDN_HEAD_DIM
    qw, vw = GDN_QH * dh, GDN_VH * dh
    nblk = s // rows
    k_blk = GDN_QK_DIM // qw
    v_blk = 2 * GDN_QK_DIM // vw
    z_blk = GDN_CONV_DIM // vw
    tail_t = tail.T
    smem = pl.BlockSpec(memory_space=pltpu.SMEM)
    return pl.pallas_call(
        _gdn_kernel,
        out_shape=jax.ShapeDtypeStruct((t, GDN_V_DIM), BF16),
        grid=(bsz, GDN_QK_HEADS // GDN_QH, nblk),
        in_specs=[
            smem, smem,
            pl.BlockSpec((rows, qw), lambda b, h, i: (b * nblk + i, h)),
            pl.BlockSpec((rows, qw), lambda b, h, i: (b * nblk + i, k_blk + h)),
            pl.BlockSpec((rows, vw), lambda b, h, i: (b * nblk + i, v_blk + h)),
            pl.BlockSpec((rows, vw), lambda b, h, i: (b * nblk + i, z_blk + h)),
            pl.BlockSpec((GDN_CONV, qw), lambda b, h, i: (0, h)),
            pl.BlockSpec((GDN_CONV, qw), lambda b, h, i: (0, k_blk + h)),
            pl.BlockSpec((GDN_CONV, vw), lambda b, h, i: (0, v_blk + h)),
            pl.BlockSpec((128, rows), lambda b, h, i: (0, b * nblk + i)),
            pl.BlockSpec((1, dh), lambda b, h, i: (0, 0)),
        ],
        out_specs=pl.BlockSpec((rows, vw), lambda b, h, i: (b * nblk + i, h)),
        scratch_shapes=[
            pltpu.VMEM((GDN_HALO, qw), F32),
            pltpu.VMEM((GDN_HALO, qw), F32),
            pltpu.VMEM((GDN_HALO, vw), F32),
            pltpu.VMEM((128, rows), F32),
            pltpu.VMEM((rows, 128), F32),
            pltpu.VMEM((GDN_VH, dh, dh), F32),
        ],
        compiler_params=pltpu.CompilerParams(
            dimension_semantics=("parallel", "parallel", "arbitrary"), vmem_limit_bytes=VMEM_LIMIT),
        name="gdn_chunk",
    )(a_log, dt_bias, proj, proj, proj, proj, conv_w, conv_w, conv_w, tail_t, norm_w.reshape(1, dh))


def kernel(x, ln_g, ln_b, ffn_w_up, ffn_w_down, gdn_w_in, gdn_conv_w, gdn_a_log, gdn_dt_bias,
           gdn_norm_w, gdn_w_out, gla_w_in, gla_w_gate, gla_gate_bias, gla_norm_w, gla_w_out,
           s5_a_re, s5_a_im, s5_b_re, s5_b_im, s5_c_re, s5_c_im, s5_d, s5_log_dt, s5_w_glu):
    bsz, s, d = x.shape
    h = x.reshape(bsz * s, d)
    w_up = ffn_w_up.astype(BF16)
    w_down = ffn_w_down.astype(BF16)
    for i in range(DEPTH):
        h = _ffn_ln(h, w_up[i, 0], w_down[i, 0], ln_g[i, 0], ln_b[i, 0])
        kind, j = i % 3, i // 3
        if kind == 0:
            n_main = GDN_CONV_DIM + GDN_V_DIM
            w_in = gdn_w_in[j].astype(BF16)
            proj = _matmul(h, w_in[:, :n_main])
            tail = _matmul(h, jnp.pad(w_in[:, n_main:], ((0, 0), (0, 128 - 2 * GDN_V_HEADS))))
            o = _gdn_core(proj, tail, gdn_conv_w[j], gdn_a_log[j], gdn_dt_bias[j], gdn_norm_w[j], bsz, s)
            h = _matmul_res_ln(o, gdn_w_out[j].astype(BF16), h, ln_g[i, 1], ln_b[i, 1])
        elif kind == 1:
            n_main = 2 * GLA_K_DIM + 2 * GLA_V_DIM
            w_in = gla_w_in[j].astype(BF16)
            proj = _matmul(h, w_in[:, :n_main])
            tail = _matmul(h, jnp.pad(w_in[:, n_main:], ((0, 0), (0, 128 - GLA_GATE_RANK))))
            o = _gla_core(proj, tail, gla_w_gate[j], gla_gate_bias[j], gla_norm_w[j], bsz, s)
            h = _matmul_res_ln(o, gla_w_out[j].astype(BF16), h, ln_g[i, 1], ln_b[i, 1])
        else:
            ht = h.reshape(bsz, s, d).transpose(1, 0, 2).reshape(s * bsz, d)
            y = _s5_core(ht, s5_a_re[j], s5_a_im[j], s5_b_re[j], s5_b_im[j], s5_c_re[j], s5_c_im[j],
                         s5_d[j], s5_log_dt[j])
            ht = _glu_res_ln(y, s5_w_glu[j].astype(BF16), ht, ln_g[i, 1], ln_b[i, 1])
            h = ht.reshape(s, bsz, d).transpose(1, 0, 2).reshape(bsz * s, d)
        h = _ffn_ln(h, w_up[i, 1], w_down[i, 1], ln_g[i, 2], ln_b[i, 2])
    return h.reshape(bsz, s, d)
```

```python
import jax
import jax.numpy as jnp
from jax import lax
from jax.experimental import pallas as pl
from jax.experimental.pallas import tpu as pltpu

F32 = jnp.float32
BF16 = jnp.bfloat16

D_MODEL = 2048
DEPTH = 4
ALPHA = (2.0 * DEPTH) ** 0.25
LN_EPS = 1e-5
RMS_EPS = 1e-6

GDN_QK_HEADS = 16
GDN_V_HEADS = 32
GDN_HEAD_DIM = 128
GDN_CONV = 4
GDN_CHUNK = 64
GDN_QK_DIM = GDN_QK_HEADS * GDN_HEAD_DIM
GDN_V_DIM = GDN_V_HEADS * GDN_HEAD_DIM
GDN_CONV_DIM = 2 * GDN_QK_DIM + GDN_V_DIM

GLA_HEADS = 4
GLA_K_DIM = D_MODEL // 2
GLA_V_DIM = D_MODEL
GLA_HEAD_K = GLA_K_DIM // GLA_HEADS
GLA_HEAD_V = GLA_V_DIM // GLA_HEADS
GLA_GATE_RANK = 16
GLA_GATE_TAU = 16.0
GLA_CHUNK = 64

S5_GROUP = 16
S5_GROUPS = D_MODEL // S5_GROUP
S5_STATE = 64

VMEM_LIMIT = 48 * 1024 * 1024


def _layer_norm(y, g, b):
    mu = jnp.mean(y, axis=-1, keepdims=True)
    yc = y - mu
    var = jnp.mean(yc * yc, axis=-1, keepdims=True)
    return yc * lax.rsqrt(var + LN_EPS) * g + b


def _softplus(x):
    return jnp.maximum(x, 0.0) + jnp.log1p(jnp.exp(-jnp.abs(x)))


def _log_sigmoid(z):
    return jnp.minimum(z, 0.0) - jnp.log1p(jnp.exp(-jnp.abs(z)))


def _silu(x):
    return x * jax.nn.sigmoid(x)


def _dot(a, b):
    return jnp.dot(a, b, preferred_element_type=F32)


def _dot_tn(a, b):
    return lax.dot_general(a, b, (((0,), (0,)), ((), ())), preferred_element_type=F32)


def _dot_nt(a, b):
    return lax.dot_general(a, b, (((1,), (1,)), ((), ())), preferred_element_type=F32)


def _split3(x):
    hi = x.astype(BF16)
    r1 = x - hi.astype(F32)
    mid = r1.astype(BF16)
    lo = (r1 - mid.astype(F32)).astype(BF16)
    return hi, mid, lo


def _ffn_kernel(x_ref, wg_ref, wu_ref, wd_ref, g_ref, b_ref, o_ref, xb_ref, acc_ref):
    f = pl.program_id(1)

    @pl.when(f == 0)
    def _():
        xb_ref[...] = x_ref[...].astype(BF16)
        acc_ref[...] = jnp.zeros_like(acc_ref)

    xb = xb_ref[...]
    gate = _dot(xb, wg_ref[...])
    up = _dot(xb, wu_ref[...])
    act = (_silu(gate) * up).astype(BF16)
    acc_ref[...] += _dot(act, wd_ref[...])

    @pl.when(f == pl.num_programs(1) - 1)
    def _():
        y = ALPHA * x_ref[...] + 0.5 * acc_ref[...]
        o_ref[...] = _layer_norm(y, g_ref[...], b_ref[...])


def _ffn_ln(x, w_up, w_down, g, b, seq, *, x_time_major=False, out_time_major=False, tm=512, tf=512):
    d = w_down.shape[1]
    t = x.size // d
    bsz = t // seq
    fdim = w_down.shape[0]
    nf = fdim // tf
    nsb = seq // tm

    def rows_map(time_major):
        if time_major:
            return lambda i, f: (i % nsb, i // nsb)
        return lambda i, f: (i, 0)

    out_shape = (seq, bsz * d) if out_time_major else (t, d)
    return pl.pallas_call(
        _ffn_kernel,
        out_shape=jax.ShapeDtypeStruct(out_shape, F32),
        grid=(t // tm, nf),
        in_specs=[
            pl.BlockSpec((tm, d), rows_map(x_time_major)),
            pl.BlockSpec((d, tf), lambda i, f: (0, f)),
            pl.BlockSpec((d, tf), lambda i, f: (0, f + nf)),
            pl.BlockSpec((tf, d), lambda i, f: (f, 0)),
            pl.BlockSpec((1, d), lambda i, f: (0, 0)),
            pl.BlockSpec((1, d), lambda i, f: (0, 0)),
        ],
        out_specs=pl.BlockSpec((tm, d), rows_map(out_time_major)),
        scratch_shapes=[pltpu.VMEM((tm, d), BF16), pltpu.VMEM((tm, d), F32)],
        compiler_params=pltpu.CompilerParams(
            dimension_semantics=("parallel", "arbitrary"), vmem_limit_bytes=VMEM_LIMIT),
        name="ffn_ln",
    )(x, w_up, w_up, w_down, g.reshape(1, d), b.reshape(1, d))


def _mm_kernel(x_ref, w_ref, o_ref):
    o_ref[...] = _dot(x_ref[...].astype(BF16), w_ref[...])


def _matmul(x, w, n=None, *, tm=1024, tn=1024):
    t, k = x.shape
    n = w.shape[1] if n is None else n
    tn = min(tn, n)
    return pl.pallas_call(
        _mm_kernel,
        out_shape=jax.ShapeDtypeStruct((t, n), F32),
        grid=(t // tm, n // tn),
        in_specs=[
            pl.BlockSpec((tm, k), lambda i, j: (i, 0)),
            pl.BlockSpec((k, tn), lambda i, j: (0, j)),
        ],
        out_specs=pl.BlockSpec((tm, tn), lambda i, j: (i, j)),
        compiler_params=pltpu.CompilerParams(
            dimension_semantics=("parallel", "arbitrary"), vmem_limit_bytes=VMEM_LIMIT),
        name="proj",
    )(x, w)


def _mm_res_ln_kernel(a_ref, w_ref, r_ref, g_ref, b_ref, o_ref, acc_ref):
    k = pl.program_id(1)

    @pl.when(k == 0)
    def _():
        acc_ref[...] = jnp.zeros_like(acc_ref)

    acc_ref[...] += _dot(a_ref[...], w_ref[...])

    @pl.when(k == pl.num_programs(1) - 1)
    def _():
        y = ALPHA * r_ref[...] + acc_ref[...]
        o_ref[...] = _layer_norm(y, g_ref[...], b_ref[...])


def _matmul_res_ln(a, w, res, g, b, *, tm=512, tk=1024):
    t, kdim = a.shape
    d = w.shape[1]
    return pl.pallas_call(
        _mm_res_ln_kernel,
        out_shape=jax.ShapeDtypeStruct((t, d), F32),
        grid=(t // tm, kdim // tk),
        in_specs=[
            pl.BlockSpec((tm, tk), lambda i, k: (i, k)),
            pl.BlockSpec((tk, d), lambda i, k: (k, 0)),
            pl.BlockSpec((tm, d), lambda i, k: (i, 0)),
            pl.BlockSpec((1, d), lambda i, k: (0, 0)),
            pl.BlockSpec((1, d), lambda i, k: (0, 0)),
        ],
        out_specs=pl.BlockSpec((tm, d), lambda i, k: (i, 0)),
        scratch_shapes=[pltpu.VMEM((tm, d), F32)],
        compiler_params=pltpu.CompilerParams(
            dimension_semantics=("parallel", "arbitrary"), vmem_limit_bytes=VMEM_LIMIT),
        name="out_proj_ln",
    )(a, w, res, g.reshape(1, d), b.reshape(1, d))


def _glu_res_ln_kernel(a_ref, wv_ref, wg_ref, r_ref, g_ref, b_ref, o_ref):
    a = a_ref[...]
    val = _dot(a, wv_ref[...])
    gate = _dot(a, wg_ref[...])
    y = ALPHA * r_ref[...] + val * jax.nn.sigmoid(gate)
    o_ref[...] = _layer_norm(y, g_ref[...], b_ref[...])


def _glu_res_ln(a, w_glu, res, g, b, *, tm=256):
    t, kdim = a.shape
    d = res.shape[1]
    return pl.pallas_call(
        _glu_res_ln_kernel,
        out_shape=jax.ShapeDtypeStruct((t, d), F32),
        grid=(t // tm,),
        in_specs=[
            pl.BlockSpec((tm, kdim), lambda i: (i, 0)),
            pl.BlockSpec((kdim, d), lambda i: (0, 0)),
            pl.BlockSpec((kdim, d), lambda i: (0, 1)),
            pl.BlockSpec((tm, d), lambda i: (i, 0)),
            pl.BlockSpec((1, d), lambda i: (0, 0)),
            pl.BlockSpec((1, d), lambda i: (0, 0)),
        ],
        out_specs=pl.BlockSpec((tm, d), lambda i: (i, 0)),
        compiler_params=pltpu.CompilerParams(
            dimension_semantics=("parallel",), vmem_limit_bytes=VMEM_LIMIT),
        name="glu_ln",
    )(a, w_glu, w_glu, res, g.reshape(1, d), b.reshape(1, d))


S5_GB = 16
S5_NGB = S5_GROUPS // S5_GB
S5_SB = S5_GB * S5_STATE
S5_BATCH = 4
S5_SUB = 128


def _s5_discretize_kernel(are_ref, aim_ref, ldt_ref, abr_ref, abi_ref, a2r_ref, a2i_ref, cfr_ref, cfi_ref):
    lr, li = are_ref[...], aim_ref[...]
    dt = jnp.exp(ldt_ref[...])
    mag = jnp.exp(lr * dt)
    ar = mag * jnp.cos(li * dt)
    ai = mag * jnp.sin(li * dt)
    abr_ref[...] = ar
    abi_ref[...] = ai
    a2r_ref[...] = ar * ar - ai * ai
    a2i_ref[...] = 2.0 * ar * ai
    nr, ni = ar - 1.0, ai
    den = lr * lr + li * li
    cfr_ref[...] = (nr * lr + ni * li) / den
    cfi_ref[...] = (ni * lr - nr * li) / den


def _s5_scan_kernel(u_ref, bm_ref, cm_ref, d_ref, a1r_ref, a1i_ref, a0r_ref, a0i_ref, o_ref,
                    hr_ref, hi_ref):
    rows = u_ref.shape[0]

    @pl.when(pl.program_id(1) == 0)
    def _():
        hr_ref[...] = jnp.zeros_like(hr_ref)
        hi_ref[...] = jnp.zeros_like(hi_ref)

    a1r, a1i = a1r_ref[0], a1i_ref[0]
    a0r, a0i = a0r_ref[0], a0i_ref[0]
    hi_rows = lax.broadcasted_iota(jnp.int32, (8, S5_SB), 0) >= S5_BATCH

    pr, pi = hr_ref[...], hi_ref[...]
    for r0 in range(0, rows, S5_SUB):
        u = u_ref[r0:r0 + S5_SUB, :]
        bu = _dot(u.astype(BF16), bm_ref[0])
        hs_r, hs_i = [], []
        for k in range(S5_SUB // 8):
            xr = bu[8 * k:8 * k + 8, 0:S5_SB]
            xi = bu[8 * k:8 * k + 8, S5_SB:2 * S5_SB]
            sr = pltpu.roll(xr, S5_BATCH, 0)
            si = pltpu.roll(xi, S5_BATCH, 0)
            hr = xr + (a0r * sr - a0i * si) + (a1r * pr - a1i * pi)
            hi = xi + (a0r * si + a0i * sr) + (a1r * pi + a1i * pr)
            hs_r.append(hr)
            hs_i.append(hi)
            pr = jnp.where(hi_rows, hr, pltpu.roll(hr, S5_BATCH, 0))
            pi = jnp.where(hi_rows, hi, pltpu.roll(hi, S5_BATCH, 0))
        states = jnp.concatenate([jnp.concatenate(hs_r, axis=0), jnp.concatenate(hs_i, axis=0)], axis=1)
        y = _dot(states.astype(BF16), cm_ref[0]) + d_ref[...] * u
        o_ref[r0:r0 + S5_SUB, :] = jax.nn.gelu(y).astype(o_ref.dtype)
    hr_ref[...] = pr
    hi_ref[...] = pi


def _s5_core(xt, a_re, a_im, b_re, b_im, c_re, c_im, d_skip, log_dt, *, ts=128):
    t, d = xt.shape
    rows = ts * S5_BATCH
    sds = jax.ShapeDtypeStruct((S5_GROUPS, S5_STATE), F32)
    abr, abi, a2r, a2i, cfr, cfi = pl.pallas_call(
        _s5_discretize_kernel, out_shape=(sds,) * 6, name="s5_discretize",
    )(a_re, a_im, log_dt.reshape(S5_GROUPS, 1))

    bbr = cfr[..., None] * b_re - cfi[..., None] * b_im
    bbi = cfr[..., None] * b_im + cfi[..., None] * b_re
    eye = jnp.eye(S5_GB, dtype=F32)

    def blockdiag_in(m):
        m = m.reshape(S5_NGB, S5_GB, S5_STATE, S5_GROUP)
        return jnp.einsum("ngph,gk->nghkp", m, eye).reshape(S5_NGB, S5_GB * S5_GROUP, S5_SB)

    def blockdiag_out(m):
        m = m.reshape(S5_NGB, S5_GB, S5_GROUP, S5_STATE)
        return jnp.einsum("nghp,gk->ngpkh", m, eye).reshape(S5_NGB, S5_SB, S5_GB * S5_GROUP)

    bmat = jnp.concatenate([blockdiag_in(bbr), blockdiag_in(bbi)], axis=-1).astype(BF16)
    cmat = jnp.concatenate([blockdiag_out(c_re), blockdiag_out(-c_im)], axis=1).astype(BF16)

    def tile8(top, bot):
        top = jnp.broadcast_to(top.reshape(S5_NGB, 1, S5_SB), (S5_NGB, S5_BATCH, S5_SB))
        bot = jnp.broadcast_to(bot.reshape(S5_NGB, 1, S5_SB), (S5_NGB, S5_BATCH, S5_SB))
        return jnp.concatenate([top, bot], axis=1)

    zero = jnp.zeros_like(abr)
    a1r, a1i = tile8(abr, a2r), tile8(abi, a2i)
    a0r, a0i = tile8(zero, abr), tile8(zero, abi)

    kin = S5_GB * S5_GROUP
    coef_spec = pl.BlockSpec((1, 8, S5_SB), lambda g, i: (g, 0, 0))
    return pl.pallas_call(
        _s5_scan_kernel,
        out_shape=jax.ShapeDtypeStruct((t, d), BF16),
        grid=(S5_NGB, t // rows),
        in_specs=[
            pl.BlockSpec((rows, kin), lambda g, i: (i, g)),
            pl.BlockSpec((1, kin, 2 * S5_SB), lambda g, i: (g, 0, 0)),
            pl.BlockSpec((1, 2 * S5_SB, kin), lambda g, i: (g, 0, 0)),
            pl.BlockSpec((1, kin), lambda g, i: (0, g)),
            coef_spec, coef_spec, coef_spec, coef_spec,
        ],
        out_specs=pl.BlockSpec((rows, kin), lambda g, i: (i, g)),
        scratch_shapes=[pltpu.VMEM((8, S5_SB), F32), pltpu.VMEM((8, S5_SB), F32)],
        compiler_params=pltpu.CompilerParams(
            dimension_semantics=("parallel", "arbitrary"), vmem_limit_bytes=VMEM_LIMIT),
        name="s5_scan",
    )(xt, bmat, cmat, d_skip.reshape(1, d), a1r, a1i, a0r, a0i)


GLA_ROWS = 512
GLA_SUB = 16


def _gla_kernel(q_ref, k_ref, v_ref, r_ref, gl_ref, wg_ref, gb_ref, nw_ref, o_ref, st_ref, f_ref, b_ref):
    c, sc, dk, dv = GLA_CHUNK, GLA_SUB, GLA_HEAD_K, GLA_HEAD_V
    heads = range(GLA_HEADS)

    @pl.when(pl.program_id(1) == 0)
    def _():
        st_ref[...] = jnp.zeros_like(st_ref)

    g_hi, g_mid, _ = _split3(gl_ref[...])
    w_hi, w_mid, _ = _split3(wg_ref[...])
    z = _dot(g_hi, w_hi) + (_dot(g_hi, w_mid) + _dot(g_mid, w_hi)) + gb_ref[...]
    f_ref[...] = _log_sigmoid(z) * (1.0 / GLA_GATE_TAU)

    row = lax.broadcasted_iota(jnp.int32, (c, c), 0)
    col = lax.broadcasted_iota(jnp.int32, (c, c), 1)
    tril = (row >= col).astype(BF16)
    sub = lax.broadcasted_iota(jnp.int32, (8, GLA_HEAD_K), 0)
    lane = lax.broadcasted_iota(jnp.int32, (8, 128), 1)

    def chunk(ci, carry):
        r0 = pl.multiple_of(ci * c, c)
        bc = sum(_dot(tril, part) for part in _split3(f_ref[pl.ds(r0, c), :]))
        b_ref[...] = bc

        tiles = [[] for _ in heads]
        for si in range(c // sc):
            base = si * sc
            qs = [q_ref[pl.ds(r0 + base, sc), h * dk:(h + 1) * dk] * (dk ** -0.5) for h in heads]
            bs = [b_ref[base:base + sc, h * dk:(h + 1) * dk] for h in heads]
            if si == 0:
                off = [jnp.zeros((sc, 128), F32) for _ in heads]
            else:
                off = []
                for h in heads:
                    ref_b = b_ref[base - 1:base, h * dk:(h + 1) * dk]
                    qh = (qs[h] * jnp.exp(bs[h] - ref_b)).astype(BF16)
                    kh = (k_ref[pl.ds(r0, base), h * dk:(h + 1) * dk]
                          * jnp.exp(ref_b - b_ref[0:base, h * dk:(h + 1) * dk])).astype(BF16)
                    kh = jnp.concatenate([kh, jnp.zeros((128 - base, dk), BF16)], axis=0)
                    off.append(_dot_nt(qh, kh))
            for h in heads:
                for t2 in range(sc // 8):
                    qi = qs[h][8 * t2:8 * t2 + 8, :]
                    bi = bs[h][8 * t2:8 * t2 + 8, :]
                    acc = off[h][8 * t2:8 * t2 + 8, :]
                    for jj in range(8 * (t2 + 1)):
                        j = base + jj
                        kj = k_ref[pl.ds(r0 + j, 1), h * dk:(h + 1) * dk]
                        bj = b_ref[j:j + 1, h * dk:(h + 1) * dk]
                        p = qi * kj * jnp.exp(bi - bj)
                        if jj >= 8 * t2:
                            p = jnp.where(sub >= jj - 8 * t2, p, 0.0)
                        s = jnp.sum(p, axis=1, keepdims=True)
                        acc = jnp.where(lane == j, s, acc)
                    tiles[h].append(acc)
        attn = [jnp.concatenate(tiles[h], axis=0)[:, :c].astype(BF16) for h in heads]

        blast = bc[c - 1:c, :]
        qe = (q_ref[pl.ds(r0, c), :] * (dk ** -0.5) * jnp.exp(bc)).astype(BF16)
        ke = (k_ref[pl.ds(r0, c), :] * jnp.exp(blast - bc)).astype(BF16)
        eb = jnp.exp(blast)
        st = [st_ref[h] for h in heads]
        v = [v_ref[pl.ds(r0, c), h * dv:(h + 1) * dv].astype(BF16) for h in heads]
        o = [_dot_nt(qe[:, h * dk:(h + 1) * dk], st[h].astype(BF16)) + _dot(attn[h], v[h]) for h in heads]
        for h in heads:
            st_ref[h] = st[h] * eb[:, h * dk:(h + 1) * dk] + _dot_tn(v[h], ke[:, h * dk:(h + 1) * dk])
        for h in heads:
            oh = o[h] * lax.rsqrt(jnp.mean(o[h] * o[h], axis=-1, keepdims=True) + RMS_EPS) * nw_ref[...]
            gate = _silu(r_ref[pl.ds(r0, c), h * dv:(h + 1) * dv])
            o_ref[pl.ds(r0, c), h * dv:(h + 1) * dv] = (oh * gate).astype(o_ref.dtype)
        return carry

    lax.fori_loop(0, q_ref.shape[0] // c, chunk, 0)


def _gla_core(proj, tail, w_gate, gate_bias, norm_w, bsz, s):
    t = bsz * s
    rows = min(GLA_ROWS, s)
    nblk = s // rows
    wg = jnp.pad(w_gate, ((0, 128 - GLA_GATE_RANK), (0, 0)))
    return pl.pallas_call(
        _gla_kernel,
        out_shape=jax.ShapeDtypeStruct((t, GLA_V_DIM), BF16),
        grid=(bsz, nblk),
        in_specs=[
            pl.BlockSpec((rows, GLA_K_DIM), lambda b, i: (b * nblk + i, 0)),
            pl.BlockSpec((rows, GLA_K_DIM), lambda b, i: (b * nblk + i, 1)),
            pl.BlockSpec((rows, GLA_V_DIM), lambda b, i: (b * nblk + i, 1)),
            pl.BlockSpec((rows, GLA_V_DIM), lambda b, i: (b * nblk + i, 2)),
            pl.BlockSpec((rows, 128), lambda b, i: (b * nblk + i, 0)),
            pl.BlockSpec((128, GLA_K_DIM), lambda b, i: (0, 0)),
            pl.BlockSpec((1, GLA_K_DIM), lambda b, i: (0, 0)),
            pl.BlockSpec((1, GLA_HEAD_V), lambda b, i: (0, 0)),
        ],
        out_specs=pl.BlockSpec((rows, GLA_V_DIM), lambda b, i: (b * nblk + i, 0)),
        scratch_shapes=[pltpu.VMEM((GLA_HEADS, GLA_HEAD_V, GLA_HEAD_K), F32),
                        pltpu.VMEM((rows, GLA_K_DIM), F32),
                        pltpu.VMEM((GLA_CHUNK, GLA_K_DIM), F32)],
        compiler_params=pltpu.CompilerParams(
            dimension_semantics=("parallel", "arbitrary"), vmem_limit_bytes=VMEM_LIMIT),
        name="gla_chunk",
    )(proj, proj, proj, proj, tail, wg, gate_bias.reshape(1, GLA_K_DIM), norm_w.reshape(1, GLA_HEAD_V))


GDN_ROWS = 256
GDN_QH = 2
GDN_VH = 2 * GDN_QH
GDN_HALO = 8


def _gdn_kernel(alog_ref, dtb_ref, q_ref, k_ref, v_ref, z_ref, cwq_ref, cwk_ref, cwv_ref, gt_ref, nw_ref,
                o_ref, xq_ref, xk_ref, xv_ref, rp_ref, qn_ref, kn_ref, vc_ref, gr_ref, cp_ref, s_ref):
    rows, c, dh, nv = GDN_ROWS, GDN_CHUNK, GDN_HEAD_DIM, GDN_VH
    nchunk = rows // c
    hp = pl.program_id(1)
    heads = range(nv)

    @pl.when(pl.program_id(2) == 0)
    def _():
        for ref in (s_ref, xq_ref, xk_ref, xv_ref, qn_ref, kn_ref, vc_ref, gr_ref, cp_ref):
            ref[...] = jnp.zeros_like(ref)

    ri = lax.broadcasted_iota(jnp.int32, (rows, rows), 0)
    ci = lax.broadcasted_iota(jnp.int32, (rows, rows), 1)
    same = (ri >> 6) == (ci >> 6)

    def conv_silu(x_ref, xs_ref, cw_ref, lo, width):
        cur = x_ref[:, lo:lo + width].reshape(rows // 8, 8, width)
        prev = jnp.concatenate([xs_ref[:, lo:lo + width].reshape(1, 8, width), cur[:-1]], axis=0)
        sub = lax.broadcasted_iota(jnp.int32, (rows // 8, 8, width), 1)
        acc = cur * cw_ref[GDN_CONV - 1:GDN_CONV, lo:lo + width]
        for s in range(1, GDN_CONV):
            shifted = pltpu.roll(jnp.where(sub >= 8 - s, prev, cur), s, 1)
            acc = acc + shifted * cw_ref[GDN_CONV - 1 - s:GDN_CONV - s, lo:lo + width]
        xs_ref[:, lo:lo + width] = x_ref[rows - GDN_HALO:rows, lo:lo + width]
        return _silu(acc).reshape(rows, width)

    def l2n(x):
        return x * lax.rsqrt(jnp.sum(x * x, axis=-1, keepdims=True) + RMS_EPS)

    nxt = {"qn": [], "kn": [], "vc": []}

    def stage_a():
        g_rows, beta_rows = [], []
        for e in heads:
            h = nv * hp + e
            b_row = gt_ref[pl.ds(h, 1), :]
            a_row = gt_ref[pl.ds(GDN_V_HEADS + h, 1), :]
            rate = jnp.exp(jnp.full((1, rows), alog_ref[h], F32))
            g_rows.append(-rate * _softplus(a_row + dtb_ref[h]))
            beta_rows.append(jax.nn.sigmoid(b_row))
        g8 = jnp.concatenate(g_rows + [jnp.zeros((8 - nv, rows), F32)], axis=0)
        sum_mats = jnp.concatenate([(same & (ri <= ci)).astype(BF16), same.astype(BF16)], axis=1)
        sums = sum(_dot(part, sum_mats) for part in _split3(g8))
        nxt["gc"] = sums[:, :rows]
        rp_ref[...] = jnp.zeros_like(rp_ref)
        rp_ref[0:8, :] = sums[:, :rows]
        for e in heads:
            rp_ref[8 + e:9 + e, :] = beta_rows[e]
        rp_ref[16:24, :] = sums[:, rows:]
        yield
        for i in range(GDN_QH):
            nxt["qn"].append(l2n(conv_silu(q_ref, xq_ref, cwq_ref, i * dh, dh)) * (dh ** -0.5))
            yield
            nxt["kn"].append(l2n(conv_silu(k_ref, xk_ref, cwk_ref, i * dh, dh)))
            yield
        for e in heads:
            nxt["vc"].append(conv_silu(v_ref, xv_ref, cwv_ref, e * dh, dh))
            yield

    prepare = stage_a()

    def advance():
        next(prepare, None)

    qn = [qn_ref[:, i * dh:(i + 1) * dh] for i in range(GDN_QH)]
    kn = [kn_ref[:, i * dh:(i + 1) * dh] for i in range(GDN_QH)]
    vc = vc_ref[...]
    gc8 = gr_ref[...]

    causal = same & (ri >= ci)
    strict = same & (ri > ci)
    in16 = ((ri >> 4) == (ci >> 4)) & (ri > ci)
    off32 = ((ri >> 5) == (ci >> 5)) & ((ri >> 4) > (ci >> 4))
    off64 = same & ((ri >> 5) > (ci >> 5))
    eye = (ri == ci).astype(F32)

    knb = [x.astype(BF16) for x in kn]
    kk = [_dot_nt(x, x) for x in knb]
    qk = [_dot_nt(qn[i].astype(BF16), knb[i]) for i in range(GDN_QH)]

    gc_col = [cp_ref[:, e:e + 1] for e in heads]
    beta_col = [cp_ref[:, 8 + e:9 + e] for e in heads]
    gt_col = [cp_ref[:, 16 + e:17 + e] for e in heads]
    decay = [jnp.where(causal, jnp.exp(gc_col[e] - gc8[e:e + 1, :]), 0.0) for e in heads]
    attn = [(qk[e // 2] * decay[e]).astype(BF16) for e in heads]
    lmat = [jnp.where(strict, kk[e // 2] * decay[e], 0.0) * beta_col[e] for e in heads]
    rhs = [jnp.concatenate([vc[:, e * dh:(e + 1) * dh] * beta_col[e],
                            kn[e // 2] * (beta_col[e] * jnp.exp(gc_col[e]))], axis=1).astype(BF16)
           for e in heads]

    y32 = [-jnp.where(in16, lm, 0.0) for lm in lmat]
    ts = [eye + y for y in y32]
    ys = [y.astype(BF16) for y in y32]
    for _ in range(3):
        ys = [_dot(y, y).astype(BF16) for y in ys]
        advance()
        ts = [t + _dot(t.astype(BF16), y) for t, y in zip(ts, ys)]
        advance()
    for off in (off32, off64):
        lo = [jnp.where(off, lm, 0.0).astype(BF16) for lm in lmat]
        tb = [t.astype(BF16) for t in ts]
        tl = [_dot(t, l).astype(BF16) for t, l in zip(tb, lo)]
        advance()
        ts = [t - _dot(a, b) for t, a, b in zip(ts, tl, tb)]
        advance()
    sol = [_dot(t.astype(BF16), r).astype(BF16) for t, r in zip(ts, rhs)]
    advance()

    awu = [_dot(a, s) for a, s in zip(attn, sol)]
    advance()
    qeff = [(qn[e // 2] * jnp.exp(gc_col[e]) - awu[e][:, dh:]).astype(BF16) for e in heads]
    kt = [(kn[e // 2] * jnp.exp(gt_col[e] - gc_col[e])).astype(BF16) for e in heads]
    kwu = [[_dot_tn(kt[e][j * c:(j + 1) * c, :], sol[e][j * c:(j + 1) * c, :]) for e in heads]
           for j in range(nchunk)]

    state = [s_ref[e] for e in heads]
    outs = [[] for _ in heads]
    for j in range(nchunk):
        sb = [s.astype(BF16) for s in state]
        for e in heads:
            outs[e].append(_dot(qeff[e][j * c:(j + 1) * c, :], sb[e]) + awu[e][j * c:(j + 1) * c, :dh])
        state = [state[e] * jnp.exp(gt_col[e][j * c:j * c + 1, :]) + kwu[j][e][:, :dh]
                 - _dot(kwu[j][e][:, dh:].astype(BF16), sb[e]) for e in heads]
    for e in heads:
        s_ref[e] = state[e]
        o = jnp.concatenate(outs[e], axis=0)
        o = o * lax.rsqrt(jnp.mean(o * o, axis=-1, keepdims=True) + RMS_EPS) * nw_ref[...]
        o_ref[:, e * dh:(e + 1) * dh] = (o * _silu(z_ref[:, e * dh:(e + 1) * dh])).astype(o_ref.dtype)

    for _ in prepare:
        pass
    qn_ref[...] = jnp.concatenate(nxt["qn"], axis=1)
    kn_ref[...] = jnp.concatenate(nxt["kn"], axis=1)
    vc_ref[...] = jnp.concatenate(nxt["vc"], axis=1)
    gr_ref[...] = nxt["gc"]
    cp_ref[...] = rp_ref[...].T


def _gdn_core(proj, tail, conv_w, a_log, dt_bias, norm_w, bsz, s):
    t = bsz * s
    rows, dh = GDN_ROWS, GDN_HEAD_DIM
    qw, vw = GDN_QH * dh, GDN_VH * dh
    nblk = s // rows
    k_blk = GDN_QK_DIM // qw
    v_blk = 2 * GDN_QK_DIM // vw
    z_blk = GDN_CONV_DIM // vw
    tail_t = tail.T
    smem = pl.BlockSpec(memory_space=pltpu.SMEM)

    def prep(b, i):
        return b * nblk + jnp.minimum(i, nblk - 1)

    def solve(b, i):
        return b * nblk + jnp.maximum(i - 1, 0)

    return pl.pallas_call(
        _gdn_kernel,
        out_shape=jax.ShapeDtypeStruct((t, GDN_V_DIM), BF16),
        grid=(bsz, GDN_QK_HEADS // GDN_QH, nblk + 1),
        in_specs=[
            smem, smem,
            pl.BlockSpec((rows, qw), lambda b, h, i: (prep(b, i), h)),
            pl.BlockSpec((rows, qw), lambda b, h, i: (prep(b, i), k_blk + h)),
            pl.BlockSpec((rows, vw), lambda b, h, i: (prep(b, i), v_blk + h)),
            pl.BlockSpec((rows, vw), lambda b, h, i: (solve(b, i), z_blk + h)),
            pl.BlockSpec((GDN_CONV, qw), lambda b, h, i: (0, h)),
            pl.BlockSpec((GDN_CONV, qw), lambda b, h, i: (0, k_blk + h)),
            pl.BlockSpec((GDN_CONV, vw), lambda b, h, i: (0, v_blk + h)),
            pl.BlockSpec((128, rows), lambda b, h, i: (0, prep(b, i))),
            pl.BlockSpec((1, dh), lambda b, h, i: (0, 0)),
        ],
        out_specs=pl.BlockSpec((rows, vw), lambda b, h, i: (solve(b, i), h)),
        scratch_shapes=[
            pltpu.VMEM((GDN_HALO, qw), F32),
            pltpu.VMEM((GDN_HALO, qw), F32),
            pltpu.VMEM((GDN_HALO, vw), F32),
            pltpu.VMEM((128, rows), F32),
            pltpu.VMEM((rows, qw), F32),
            pltpu.VMEM((rows, qw), F32),
            pltpu.VMEM((rows, vw), F32),
            pltpu.VMEM((8, rows), F32),
            pltpu.VMEM((rows, 128), F32),
            pltpu.VMEM((GDN_VH, dh, dh), F32),
        ],
        compiler_params=pltpu.CompilerParams(
            dimension_semantics=("parallel", "parallel", "arbitrary"), vmem_limit_bytes=VMEM_LIMIT),
        name="gdn_chunk",
    )(a_log, dt_bias, proj, proj, proj, proj, conv_w, conv_w, conv_w, tail_t, norm_w.reshape(1, dh))


def kernel(x, ln_g, ln_b, ffn_w_up, ffn_w_down, gdn_w_in, gdn_conv_w, gdn_a_log, gdn_dt_bias,
           gdn_norm_w, gdn_w_out, gla_w_in, gla_w_gate, gla_gate_bias, gla_norm_w, gla_w_out,
           s5_a_re, s5_a_im, s5_b_re, s5_b_im, s5_c_re, s5_c_im, s5_d, s5_log_dt, s5_w_glu):
    bsz, s, d = x.shape
    h = x.reshape(bsz * s, d)
    w_up = ffn_w_up.astype(BF16)
    w_down = ffn_w_down.astype(BF16)
    for i in range(DEPTH):
        kind, j = i % 3, i // 3
        s5_layer = kind == 2
        h = _ffn_ln(h, w_up[i, 0], w_down[i, 0], ln_g[i, 0], ln_b[i, 0], s, out_time_major=s5_layer)
        if kind == 0:
            n_main = GDN_CONV_DIM + GDN_V_DIM
            w_in = gdn_w_in[j].astype(BF16)
            proj = _matmul(h, w_in, n_main)
            tail = _matmul(h, jnp.pad(w_in[:, n_main:], ((0, 0), (0, 128 - 2 * GDN_V_HEADS))))
            o = _gdn_core(proj, tail, gdn_conv_w[j], gdn_a_log[j], gdn_dt_bias[j], gdn_norm_w[j], bsz, s)
            h = _matmul_res_ln(o, gdn_w_out[j].astype(BF16), h, ln_g[i, 1], ln_b[i, 1])
        elif kind == 1:
            n_main = 2 * GLA_K_DIM + 2 * GLA_V_DIM
            w_in = gla_w_in[j].astype(BF16)
            proj = _matmul(h, w_in, n_main)
            tail = _matmul(h, jnp.pad(w_in[:, n_main:], ((0, 0), (0, 128 - GLA_GATE_RANK))))
            o = _gla_core(proj, tail, gla_w_gate[j], gla_gate_bias[j], gla_norm_w[j], bsz, s)
            h = _matmul_res_ln(o, gla_w_out[j].astype(BF16), h, ln_g[i, 1], ln_b[i, 1])
        else:
            ht = h.reshape(s * bsz, d)
            y = _s5_core(ht, s5_a_re[j], s5_a_im[j], s5_b_re[j], s5_b_im[j], s5_c_re[j], s5_c_im[j],
                         s5_d[j], s5_log_dt[j])
            ht = _glu_res_ln(y, s5_w_glu[j].astype(BF16), ht, ln_g[i, 1], ln_b[i, 1])
            h = ht.reshape(s, bsz * d)
        h = _ffn_ln(h, w_up[i, 1], w_down[i, 1], ln_g[i, 2], ln_b[i, 2], s, x_time_major=s5_layer)
    return h.reshape(bsz, s, d)
```

```python
import jax
import jax.numpy as jnp
from jax import lax
from jax.experimental import pallas as pl
from jax.experimental.pallas import tpu as pltpu

F32 = jnp.float32
BF16 = jnp.bfloat16

D_MODEL = 2048
DEPTH = 4
ALPHA = (2.0 * DEPTH) ** 0.25
LN_EPS = 1e-5
RMS_EPS = 1e-6

GDN_QK_HEADS = 16
GDN_V_HEADS = 32
GDN_HEAD_DIM = 128
GDN_CONV = 4
GDN_CHUNK = 64
GDN_QK_DIM = GDN_QK_HEADS * GDN_HEAD_DIM
GDN_V_DIM = GDN_V_HEADS * GDN_HEAD_DIM
GDN_CONV_DIM = 2 * GDN_QK_DIM + GDN_V_DIM

GLA_HEADS = 4
GLA_K_DIM = D_MODEL // 2
GLA_V_DIM = D_MODEL
GLA_HEAD_K = GLA_K_DIM // GLA_HEADS
GLA_HEAD_V = GLA_V_DIM // GLA_HEADS
GLA_GATE_RANK = 16
GLA_GATE_TAU = 16.0
GLA_CHUNK = 64

S5_GROUP = 16
S5_GROUPS = D_MODEL // S5_GROUP
S5_STATE = 64

VMEM_LIMIT = 48 * 1024 * 1024


def _layer_norm(y, g, b):
    mu = jnp.mean(y, axis=-1, keepdims=True)
    yc = y - mu
    var = jnp.mean(yc * yc, axis=-1, keepdims=True)
    return yc * lax.rsqrt(var + LN_EPS) * g + b


def _softplus(x):
    return jnp.maximum(x, 0.0) + jnp.log1p(jnp.exp(-jnp.abs(x)))


def _log_sigmoid(z):
    return jnp.minimum(z, 0.0) - jnp.log1p(jnp.exp(-jnp.abs(z)))


def _silu(x):
    return x * jax.nn.sigmoid(x)


def _dot(a, b):
    return jnp.dot(a, b, preferred_element_type=F32)


def _dot_tn(a, b):
    return lax.dot_general(a, b, (((0,), (0,)), ((), ())), preferred_element_type=F32)


def _dot_nt(a, b):
    return lax.dot_general(a, b, (((1,), (1,)), ((), ())), preferred_element_type=F32)


def _split3(x):
    hi = x.astype(BF16)
    r1 = x - hi.astype(F32)
    mid = r1.astype(BF16)
    lo = (r1 - mid.astype(F32)).astype(BF16)
    return hi, mid, lo


def _ffn_kernel(x_ref, wg_ref, wu_ref, wd_ref, g_ref, b_ref, o_ref, xb_ref, acc_ref):
    f = pl.program_id(1)

    @pl.when(f == 0)
    def _():
        xb_ref[...] = x_ref[...].astype(BF16)
        acc_ref[...] = jnp.zeros_like(acc_ref)

    xb = xb_ref[...]
    gate = _dot(xb, wg_ref[...])
    up = _dot(xb, wu_ref[...])
    act = (_silu(gate) * up).astype(BF16)
    acc_ref[...] += _dot(act, wd_ref[...])

    @pl.when(f == pl.num_programs(1) - 1)
    def _():
        y = ALPHA * x_ref[...] + 0.5 * acc_ref[...]
        o_ref[...] = _layer_norm(y, g_ref[...], b_ref[...])


def _ffn_ln(x, w_up, w_down, g, b, seq, *, x_time_major=False, out_time_major=False, tm=512, tf=512):
    d = w_down.shape[1]
    t = x.size // d
    bsz = t // seq
    fdim = w_down.shape[0]
    nf = fdim // tf
    nsb = seq // tm

    def rows_map(time_major):
        if time_major:
            return lambda i, f: (i % nsb, i // nsb)
        return lambda i, f: (i, 0)

    out_shape = (seq, bsz * d) if out_time_major else (t, d)
    return pl.pallas_call(
        _ffn_kernel,
        out_shape=jax.ShapeDtypeStruct(out_shape, F32),
        grid=(t // tm, nf),
        in_specs=[
            pl.BlockSpec((tm, d), rows_map(x_time_major)),
            pl.BlockSpec((d, tf), lambda i, f: (0, f)),
            pl.BlockSpec((d, tf), lambda i, f: (0, f + nf)),
            pl.BlockSpec((tf, d), lambda i, f: (f, 0)),
            pl.BlockSpec((1, d), lambda i, f: (0, 0)),
            pl.BlockSpec((1, d), lambda i, f: (0, 0)),
        ],
        out_specs=pl.BlockSpec((tm, d), rows_map(out_time_major)),
        scratch_shapes=[pltpu.VMEM((tm, d), BF16), pltpu.VMEM((tm, d), F32)],
        compiler_params=pltpu.CompilerParams(
            dimension_semantics=("parallel", "arbitrary"), vmem_limit_bytes=VMEM_LIMIT),
        name="ffn_ln",
    )(x, w_up, w_up, w_down, g.reshape(1, d), b.reshape(1, d))


def _mm_kernel(x_ref, w_ref, o_ref):
    o_ref[...] = _dot(x_ref[...].astype(BF16), w_ref[...])


def _matmul(x, w, n=None, *, tm=1024, tn=1024):
    t, k = x.shape
    n = w.shape[1] if n is None else n
    tn = min(tn, n)
    return pl.pallas_call(
        _mm_kernel,
        out_shape=jax.ShapeDtypeStruct((t, n), F32),
        grid=(t // tm, n // tn),
        in_specs=[
            pl.BlockSpec((tm, k), lambda i, j: (i, 0)),
            pl.BlockSpec((k, tn), lambda i, j: (0, j)),
        ],
        out_specs=pl.BlockSpec((tm, tn), lambda i, j: (i, j)),
        compiler_params=pltpu.CompilerParams(
            dimension_semantics=("parallel", "arbitrary"), vmem_limit_bytes=VMEM_LIMIT),
        name="proj",
    )(x, w)


def _mm_res_ln_kernel(a_ref, w_ref, r_ref, g_ref, b_ref, o_ref):
    y = ALPHA * r_ref[...] + _dot(a_ref[...], w_ref[...])
    o_ref[...] = _layer_norm(y, g_ref[...], b_ref[...])


def _matmul_res_ln(a, w, res, g, b, *, tm=512):
    t, kdim = a.shape
    d = w.shape[1]
    return pl.pallas_call(
        _mm_res_ln_kernel,
        out_shape=jax.ShapeDtypeStruct((t, d), F32),
        grid=(t // tm,),
        in_specs=[
            pl.BlockSpec((tm, kdim), lambda i: (i, 0)),
            pl.BlockSpec((kdim, d), lambda i: (0, 0), pipeline_mode=pl.Buffered(1)),
            pl.BlockSpec((tm, d), lambda i: (i, 0)),
            pl.BlockSpec((1, d), lambda i: (0, 0)),
            pl.BlockSpec((1, d), lambda i: (0, 0)),
        ],
        out_specs=pl.BlockSpec((tm, d), lambda i: (i, 0)),
        compiler_params=pltpu.CompilerParams(
            dimension_semantics=("parallel",), vmem_limit_bytes=VMEM_LIMIT),
        name="out_proj_ln",
    )(a, w, res, g.reshape(1, d), b.reshape(1, d))


def _glu_res_ln_kernel(a_ref, wv_ref, wg_ref, r_ref, g_ref, b_ref, o_ref):
    a = a_ref[...]
    val = _dot(a, wv_ref[...])
    gate = _dot(a, wg_ref[...])
    y = ALPHA * r_ref[...] + val * jax.nn.sigmoid(gate)
    o_ref[...] = _layer_norm(y, g_ref[...], b_ref[...])


def _glu_res_ln(a, w_glu, res, g, b, *, tm=512):
    t, kdim = a.shape
    d = res.shape[1]
    once = pl.Buffered(1)
    return pl.pallas_call(
        _glu_res_ln_kernel,
        out_shape=jax.ShapeDtypeStruct((t, d), F32),
        grid=(t // tm,),
        in_specs=[
            pl.BlockSpec((tm, kdim), lambda i: (i, 0)),
            pl.BlockSpec((kdim, d), lambda i: (0, 0), pipeline_mode=once),
            pl.BlockSpec((kdim, d), lambda i: (0, 1), pipeline_mode=once),
            pl.BlockSpec((tm, d), lambda i: (i, 0)),
            pl.BlockSpec((1, d), lambda i: (0, 0)),
            pl.BlockSpec((1, d), lambda i: (0, 0)),
        ],
        out_specs=pl.BlockSpec((tm, d), lambda i: (i, 0)),
        compiler_params=pltpu.CompilerParams(
            dimension_semantics=("parallel",), vmem_limit_bytes=VMEM_LIMIT),
        name="glu_ln",
    )(a, w_glu, w_glu, res, g.reshape(1, d), b.reshape(1, d))


S5_GB = 16
S5_NGB = S5_GROUPS // S5_GB
S5_SB = S5_GB * S5_STATE
S5_BATCH = 4
S5_SUB = 128


def _s5_discretize_kernel(are_ref, aim_ref, ldt_ref, abr_ref, abi_ref, a2r_ref, a2i_ref, cfr_ref, cfi_ref):
    lr, li = are_ref[...], aim_ref[...]
    dt = jnp.exp(ldt_ref[...])
    mag = jnp.exp(lr * dt)
    ar = mag * jnp.cos(li * dt)
    ai = mag * jnp.sin(li * dt)
    abr_ref[...] = ar
    abi_ref[...] = ai
    a2r_ref[...] = ar * ar - ai * ai
    a2i_ref[...] = 2.0 * ar * ai
    nr, ni = ar - 1.0, ai
    den = lr * lr + li * li
    cfr_ref[...] = (nr * lr + ni * li) / den
    cfi_ref[...] = (ni * lr - nr * li) / den


def _s5_scan_kernel(u_ref, bm_ref, cm_ref, d_ref, a1r_ref, a1i_ref, a0r_ref, a0i_ref, o_ref,
                    hr_ref, hi_ref):
    rows = u_ref.shape[0]

    @pl.when(pl.program_id(1) == 0)
    def _():
        hr_ref[...] = jnp.zeros_like(hr_ref)
        hi_ref[...] = jnp.zeros_like(hi_ref)

    a1r, a1i = a1r_ref[0], a1i_ref[0]
    a0r, a0i = a0r_ref[0], a0i_ref[0]
    hi_rows = lax.broadcasted_iota(jnp.int32, (8, S5_SB), 0) >= S5_BATCH

    pr, pi = hr_ref[...], hi_ref[...]
    for r0 in range(0, rows, S5_SUB):
        u = u_ref[r0:r0 + S5_SUB, :]
        bu = _dot(u.astype(BF16), bm_ref[0])
        hs_r, hs_i = [], []
        for k in range(S5_SUB // 8):
            xr = bu[8 * k:8 * k + 8, 0:S5_SB]
            xi = bu[8 * k:8 * k + 8, S5_SB:2 * S5_SB]
            sr = pltpu.roll(xr, S5_BATCH, 0)
            si = pltpu.roll(xi, S5_BATCH, 0)
            hr = xr + (a0r * sr - a0i * si) + (a1r * pr - a1i * pi)
            hi = xi + (a0r * si + a0i * sr) + (a1r * pi + a1i * pr)
            hs_r.append(hr)
            hs_i.append(hi)
            pr = jnp.where(hi_rows, hr, pltpu.roll(hr, S5_BATCH, 0))
            pi = jnp.where(hi_rows, hi, pltpu.roll(hi, S5_BATCH, 0))
        states = jnp.concatenate([jnp.concatenate(hs_r, axis=0), jnp.concatenate(hs_i, axis=0)], axis=1)
        y = _dot(states.astype(BF16), cm_ref[0]) + d_ref[...] * u
        o_ref[r0:r0 + S5_SUB, :] = jax.nn.gelu(y).astype(o_ref.dtype)
    hr_ref[...] = pr
    hi_ref[...] = pi


def _s5_core(xt, a_re, a_im, b_re, b_im, c_re, c_im, d_skip, log_dt, *, ts=128):
    t, d = xt.shape
    rows = ts * S5_BATCH
    sds = jax.ShapeDtypeStruct((S5_GROUPS, S5_STATE), F32)
    abr, abi, a2r, a2i, cfr, cfi = pl.pallas_call(
        _s5_discretize_kernel, out_shape=(sds,) * 6, name="s5_discretize",
    )(a_re, a_im, log_dt.reshape(S5_GROUPS, 1))

    bbr = cfr[..., None] * b_re - cfi[..., None] * b_im
    bbi = cfr[..., None] * b_im + cfi[..., None] * b_re
    eye = jnp.eye(S5_GB, dtype=F32)

    def blockdiag_in(m):
        m = m.reshape(S5_NGB, S5_GB, S5_STATE, S5_GROUP)
        return jnp.einsum("ngph,gk->nghkp", m, eye).reshape(S5_NGB, S5_GB * S5_GROUP, S5_SB)

    def blockdiag_out(m):
        m = m.reshape(S5_NGB, S5_GB, S5_GROUP, S5_STATE)
        return jnp.einsum("nghp,gk->ngpkh", m, eye).reshape(S5_NGB, S5_SB, S5_GB * S5_GROUP)

    bmat = jnp.concatenate([blockdiag_in(bbr), blockdiag_in(bbi)], axis=-1).astype(BF16)
    cmat = jnp.concatenate([blockdiag_out(c_re), blockdiag_out(-c_im)], axis=1).astype(BF16)

    def tile8(top, bot):
        top = jnp.broadcast_to(top.reshape(S5_NGB, 1, S5_SB), (S5_NGB, S5_BATCH, S5_SB))
        bot = jnp.broadcast_to(bot.reshape(S5_NGB, 1, S5_SB), (S5_NGB, S5_BATCH, S5_SB))
        return jnp.concatenate([top, bot], axis=1)

    zero = jnp.zeros_like(abr)
    a1r, a1i = tile8(abr, a2r), tile8(abi, a2i)
    a0r, a0i = tile8(zero, abr), tile8(zero, abi)

    kin = S5_GB * S5_GROUP
    coef_spec = pl.BlockSpec((1, 8, S5_SB), lambda g, i: (g, 0, 0))
    return pl.pallas_call(
        _s5_scan_kernel,
        out_shape=jax.ShapeDtypeStruct((t, d), BF16),
        grid=(S5_NGB, t // rows),
        in_specs=[
            pl.BlockSpec((rows, kin), lambda g, i: (i, g)),
            pl.BlockSpec((1, kin, 2 * S5_SB), lambda g, i: (g, 0, 0)),
            pl.BlockSpec((1, 2 * S5_SB, kin), lambda g, i: (g, 0, 0)),
            pl.BlockSpec((1, kin), lambda g, i: (0, g)),
            coef_spec, coef_spec, coef_spec, coef_spec,
        ],
        out_specs=pl.BlockSpec((rows, kin), lambda g, i: (i, g)),
        scratch_shapes=[pltpu.VMEM((8, S5_SB), F32), pltpu.VMEM((8, S5_SB), F32)],
        compiler_params=pltpu.CompilerParams(
            dimension_semantics=("parallel", "arbitrary"), vmem_limit_bytes=VMEM_LIMIT),
        name="s5_scan",
    )(xt, bmat, cmat, d_skip.reshape(1, d), a1r, a1i, a0r, a0i)


GLA_ROWS = 512
GLA_SUB = 16


def _gla_kernel(q_ref, k_ref, v_ref, r_ref, gl_ref, wg_ref, gb_ref, nw_ref, o_ref, st_ref, f_ref, b_ref):
    c, sc, dk, dv = GLA_CHUNK, GLA_SUB, GLA_HEAD_K, GLA_HEAD_V
    heads = range(GLA_HEADS)

    @pl.when(pl.program_id(1) == 0)
    def _():
        st_ref[...] = jnp.zeros_like(st_ref)

    g_hi, g_mid, _ = _split3(gl_ref[...])
    w_hi, w_mid, _ = _split3(wg_ref[...])
    z = _dot(g_hi, w_hi) + (_dot(g_hi, w_mid) + _dot(g_mid, w_hi)) + gb_ref[...]
    f_ref[...] = _log_sigmoid(z) * (1.0 / GLA_GATE_TAU)

    row = lax.broadcasted_iota(jnp.int32, (c, c), 0)
    col = lax.broadcasted_iota(jnp.int32, (c, c), 1)
    tril = (row >= col).astype(BF16)
    sub = lax.broadcasted_iota(jnp.int32, (8, GLA_HEAD_K), 0)
    lane = lax.broadcasted_iota(jnp.int32, (8, 128), 1)

    def chunk(ci, carry):
        r0 = pl.multiple_of(ci * c, c)
        bc = sum(_dot(tril, part) for part in _split3(f_ref[pl.ds(r0, c), :]))
        b_ref[...] = bc

        tiles = [[] for _ in heads]
        for si in range(c // sc):
            base = si * sc
            qs = [q_ref[pl.ds(r0 + base, sc), h * dk:(h + 1) * dk] * (dk ** -0.5) for h in heads]
            bs = [b_ref[base:base + sc, h * dk:(h + 1) * dk] for h in heads]
            if si == 0:
                off = [jnp.zeros((sc, 128), F32) for _ in heads]
            else:
                off = []
                for h in heads:
                    ref_b = b_ref[base - 1:base, h * dk:(h + 1) * dk]
                    qh = (qs[h] * jnp.exp(bs[h] - ref_b)).astype(BF16)
                    kh = (k_ref[pl.ds(r0, base), h * dk:(h + 1) * dk]
                          * jnp.exp(ref_b - b_ref[0:base, h * dk:(h + 1) * dk])).astype(BF16)
                    kh = jnp.concatenate([kh, jnp.zeros((128 - base, dk), BF16)], axis=0)
                    off.append(_dot_nt(qh, kh))
            for h in heads:
                for t2 in range(sc // 8):
                    qi = qs[h][8 * t2:8 * t2 + 8, :]
                    bi = bs[h][8 * t2:8 * t2 + 8, :]
                    acc = off[h][8 * t2:8 * t2 + 8, :]
                    for jj in range(8 * (t2 + 1)):
                        j = base + jj
                        kj = k_ref[pl.ds(r0 + j, 1), h * dk:(h + 1) * dk]
                        bj = b_ref[j:j + 1, h * dk:(h + 1) * dk]
                        p = qi * kj * jnp.exp(bi - bj)
                        if jj >= 8 * t2:
                            p = jnp.where(sub >= jj - 8 * t2, p, 0.0)
                        s = jnp.sum(p, axis=1, keepdims=True)
                        acc = jnp.where(lane == j, s, acc)
                    tiles[h].append(acc)
        attn = [jnp.concatenate(tiles[h], axis=0)[:, :c].astype(BF16) for h in heads]

        blast = bc[c - 1:c, :]
        qe = (q_ref[pl.ds(r0, c), :] * (dk ** -0.5) * jnp.exp(bc)).astype(BF16)
        ke = (k_ref[pl.ds(r0, c), :] * jnp.exp(blast - bc)).astype(BF16)
        eb = jnp.exp(blast)
        st = [st_ref[h] for h in heads]
        v = [v_ref[pl.ds(r0, c), h * dv:(h + 1) * dv].astype(BF16) for h in heads]
        o = [_dot_nt(qe[:, h * dk:(h + 1) * dk], st[h].astype(BF16)) + _dot(attn[h], v[h]) for h in heads]
        for h in heads:
            st_ref[h] = st[h] * eb[:, h * dk:(h + 1) * dk] + _dot_tn(v[h], ke[:, h * dk:(h + 1) * dk])
        for h in heads:
            oh = o[h] * lax.rsqrt(jnp.mean(o[h] * o[h], axis=-1, keepdims=True) + RMS_EPS) * nw_ref[...]
            gate = _silu(r_ref[pl.ds(r0, c), h * dv:(h + 1) * dv])
            o_ref[pl.ds(r0, c), h * dv:(h + 1) * dv] = (oh * gate).astype(o_ref.dtype)
        return carry

    lax.fori_loop(0, q_ref.shape[0] // c, chunk, 0)


def _gla_core(proj, tail, w_gate, gate_bias, norm_w, bsz, s):
    t = bsz * s
    rows = min(GLA_ROWS, s)
    nblk = s // rows
    wg = jnp.pad(w_gate, ((0, 128 - GLA_GATE_RANK), (0, 0)))
    return pl.pallas_call(
        _gla_kernel,
        out_shape=jax.ShapeDtypeStruct((t, GLA_V_DIM), BF16),
        grid=(bsz, nblk),
        in_specs=[
            pl.BlockSpec((rows, GLA_K_DIM), lambda b, i: (b * nblk + i, 0)),
            pl.BlockSpec((rows, GLA_K_DIM), lambda b, i: (b * nblk + i, 1)),
            pl.BlockSpec((rows, GLA_V_DIM), lambda b, i: (b * nblk + i, 1)),
            pl.BlockSpec((rows, GLA_V_DIM), lambda b, i: (b * nblk + i, 2)),
            pl.BlockSpec((rows, 128), lambda b, i: (b * nblk + i, 0)),
            pl.BlockSpec((128, GLA_K_DIM), lambda b, i: (0, 0)),
            pl.BlockSpec((1, GLA_K_DIM), lambda b, i: (0, 0)),
            pl.BlockSpec((1, GLA_HEAD_V), lambda b, i: (0, 0)),
        ],
        out_specs=pl.BlockSpec((rows, GLA_V_DIM), lambda b, i: (b * nblk + i, 0)),
        scratch_shapes=[pltpu.VMEM((GLA_HEADS, GLA_HEAD_V, GLA_HEAD_K), F32),
                        pltpu.VMEM((rows, GLA_K_DIM), F32),
                        pltpu.VMEM((GLA_CHUNK, GLA_K_DIM), F32)],
        compiler_params=pltpu.CompilerParams(
            dimension_semantics=("parallel", "arbitrary"), vmem_limit_bytes=VMEM_LIMIT),
        name="gla_chunk",
    )(proj, proj, proj, proj, tail, wg, gate_bias.reshape(1, GLA_K_DIM), norm_w.reshape(1, GLA_HEAD_V))


GDN_ROWS = 256
GDN_QH = 4
GDN_VH = 2 * GDN_QH
GDN_HALO = 8


def _gdn_kernel(alog_ref, dtb_ref, q_ref, k_ref, v_ref, z_ref, cwq_ref, cwk_ref, cwv_ref, gt_ref, nw_ref,
                o_ref, xq_ref, xk_ref, xv_ref, rp_ref, qn_ref, kn_ref, vc_ref, gr_ref, cp_ref, s_ref):
    rows, c, dh, nv = GDN_ROWS, GDN_CHUNK, GDN_HEAD_DIM, GDN_VH
    nchunk = rows // c
    hp = pl.program_id(1)
    heads = range(nv)

    @pl.when(pl.program_id(2) == 0)
    def _():
        for ref in (s_ref, xq_ref, xk_ref, xv_ref, qn_ref, kn_ref, vc_ref, gr_ref, cp_ref):
            ref[...] = jnp.zeros_like(ref)

    ri = lax.broadcasted_iota(jnp.int32, (rows, rows), 0)
    ci = lax.broadcasted_iota(jnp.int32, (rows, rows), 1)
    same = (ri >> 6) == (ci >> 6)

    def conv_silu(x_ref, xs_ref, cw_ref, lo, width):
        cur = x_ref[:, lo:lo + width].reshape(rows // 8, 8, width)
        prev = jnp.concatenate([xs_ref[:, lo:lo + width].reshape(1, 8, width), cur[:-1]], axis=0)
        sub = lax.broadcasted_iota(jnp.int32, (rows // 8, 8, width), 1)
        acc = cur * cw_ref[GDN_CONV - 1:GDN_CONV, lo:lo + width]
        for s in range(1, GDN_CONV):
            shifted = pltpu.roll(jnp.where(sub >= 8 - s, prev, cur), s, 1)
            acc = acc + shifted * cw_ref[GDN_CONV - 1 - s:GDN_CONV - s, lo:lo + width]
        xs_ref[:, lo:lo + width] = x_ref[rows - GDN_HALO:rows, lo:lo + width]
        return _silu(acc).reshape(rows, width)

    def l2n(x):
        return x * lax.rsqrt(jnp.sum(x * x, axis=-1, keepdims=True) + RMS_EPS)

    nxt = {"qn": [], "kn": [], "vc": []}

    def stage_a():
        g_rows, beta_rows = [], []
        for e in heads:
            h = nv * hp + e
            b_row = gt_ref[pl.ds(h, 1), :]
            a_row = gt_ref[pl.ds(GDN_V_HEADS + h, 1), :]
            rate = jnp.exp(jnp.full((1, rows), alog_ref[h], F32))
            g_rows.append(-rate * _softplus(a_row + dtb_ref[h]))
            beta_rows.append(jax.nn.sigmoid(b_row))
        g8 = jnp.concatenate(g_rows + [jnp.zeros((1, rows), F32)] * (8 - nv), axis=0)
        sum_mats = jnp.concatenate([(same & (ri <= ci)).astype(BF16), same.astype(BF16)], axis=1)
        sums = sum(_dot(part, sum_mats) for part in _split3(g8))
        nxt["gc"] = sums[:, :rows]
        rp_ref[...] = jnp.zeros_like(rp_ref)
        rp_ref[0:8, :] = sums[:, :rows]
        for e in heads:
            rp_ref[8 + e:9 + e, :] = beta_rows[e]
        rp_ref[16:24, :] = sums[:, rows:]
        yield
        for i in range(GDN_QH):
            nxt["qn"].append(l2n(conv_silu(q_ref, xq_ref, cwq_ref, i * dh, dh)) * (dh ** -0.5))
            yield
            nxt["kn"].append(l2n(conv_silu(k_ref, xk_ref, cwk_ref, i * dh, dh)))
            yield
        for e in heads:
            nxt["vc"].append(conv_silu(v_ref, xv_ref, cwv_ref, e * dh, dh))
            yield

    prepare = stage_a()

    def advance():
        next(prepare, None)

    qn = [qn_ref[:, i * dh:(i + 1) * dh] for i in range(GDN_QH)]
    kn = [kn_ref[:, i * dh:(i + 1) * dh] for i in range(GDN_QH)]
    vc = vc_ref[...]
    gc8 = gr_ref[...]

    causal = same & (ri >= ci)
    strict = same & (ri > ci)
    in16 = ((ri >> 4) == (ci >> 4)) & (ri > ci)
    off32 = ((ri >> 5) == (ci >> 5)) & ((ri >> 4) > (ci >> 4))
    off64 = same & ((ri >> 5) > (ci >> 5))
    eye = (ri == ci).astype(F32)

    knb = [x.astype(BF16) for x in kn]
    kk = [_dot_nt(x, x) for x in knb]
    qk = [_dot_nt(qn[i].astype(BF16), knb[i]) for i in range(GDN_QH)]

    gc_col = [cp_ref[:, e:e + 1] for e in heads]
    beta_col = [cp_ref[:, 8 + e:9 + e] for e in heads]
    gt_col = [cp_ref[:, 16 + e:17 + e] for e in heads]
    decay = [jnp.where(causal, jnp.exp(gc_col[e] - gc8[e:e + 1, :]), 0.0) for e in heads]
    attn = [(qk[e // 2] * decay[e]).astype(BF16) for e in heads]
    lmat = [jnp.where(strict, kk[e // 2] * decay[e], 0.0) * beta_col[e] for e in heads]
    rhs = [jnp.concatenate([vc[:, e * dh:(e + 1) * dh] * beta_col[e],
                            kn[e // 2] * (beta_col[e] * jnp.exp(gc_col[e]))], axis=1).astype(BF16)
           for e in heads]

    y32 = [-jnp.where(in16, lm, 0.0) for lm in lmat]
    ts = [eye + y for y in y32]
    ys = [y.astype(BF16) for y in y32]
    for _ in range(3):
        ys = [_dot(y, y).astype(BF16) for y in ys]
        advance()
        ts = [t + _dot(t.astype(BF16), y) for t, y in zip(ts, ys)]
        advance()
    for off in (off32, off64):
        lo = [jnp.where(off, lm, 0.0).astype(BF16) for lm in lmat]
        tb = [t.astype(BF16) for t in ts]
        tl = [_dot(t, l).astype(BF16) for t, l in zip(tb, lo)]
        advance()
        ts = [t - _dot(a, b) for t, a, b in zip(ts, tl, tb)]
        advance()
    sol = [_dot(t.astype(BF16), r).astype(BF16) for t, r in zip(ts, rhs)]
    advance()

    awu = [_dot(a, s) for a, s in zip(attn, sol)]
    advance()
    qeff = [(qn[e // 2] * jnp.exp(gc_col[e]) - awu[e][:, dh:]).astype(BF16) for e in heads]
    kt = [(kn[e // 2] * jnp.exp(gt_col[e] - gc_col[e])).astype(BF16) for e in heads]
    kwu = [[_dot_tn(kt[e][j * c:(j + 1) * c, :], sol[e][j * c:(j + 1) * c, :]) for e in heads]
           for j in range(nchunk)]

    state = [s_ref[e] for e in heads]
    outs = [[] for _ in heads]
    for j in range(nchunk):
        sb = [s.astype(BF16) for s in state]
        for e in heads:
            outs[e].append(_dot(qeff[e][j * c:(j + 1) * c, :], sb[e]) + awu[e][j * c:(j + 1) * c, :dh])
        state = [state[e] * jnp.exp(gt_col[e][j * c:j * c + 1, :]) + kwu[j][e][:, :dh]
                 - _dot(kwu[j][e][:, dh:].astype(BF16), sb[e]) for e in heads]
    for e in heads:
        s_ref[e] = state[e]
        o = jnp.concatenate(outs[e], axis=0)
        o = o * lax.rsqrt(jnp.mean(o * o, axis=-1, keepdims=True) + RMS_EPS) * nw_ref[...]
        o_ref[:, e * dh:(e + 1) * dh] = (o * _silu(z_ref[:, e * dh:(e + 1) * dh])).astype(o_ref.dtype)

    for _ in prepare:
        pass
    qn_ref[...] = jnp.concatenate(nxt["qn"], axis=1)
    kn_ref[...] = jnp.concatenate(nxt["kn"], axis=1)
    vc_ref[...] = jnp.concatenate(nxt["vc"], axis=1)
    gr_ref[...] = nxt["gc"]
    cp_ref[...] = rp_ref[...].T


def _gdn_core(proj, tail, conv_w, a_log, dt_bias, norm_w, bsz, s):
    t = bsz * s
    rows, dh = GDN_ROWS, GDN_HEAD_DIM
    qw, vw = GDN_QH * dh, GDN_VH * dh
    nblk = s // rows
    k_blk = GDN_QK_DIM // qw
    v_blk = 2 * GDN_QK_DIM // vw
    z_blk = GDN_CONV_DIM // vw
    tail_t = tail.T
    smem = pl.BlockSpec(memory_space=pltpu.SMEM)

    def prep(b, i):
        return b * nblk + jnp.minimum(i, nblk - 1)

    def solve(b, i):
        return b * nblk + jnp.maximum(i - 1, 0)

    return pl.pallas_call(
        _gdn_kernel,
        out_shape=jax.ShapeDtypeStruct((t, GDN_V_DIM), BF16),
        grid=(bsz, GDN_QK_HEADS // GDN_QH, nblk + 1),
        in_specs=[
            smem, smem,
            pl.BlockSpec((rows, qw), lambda b, h, i: (prep(b, i), h)),
            pl.BlockSpec((rows, qw), lambda b, h, i: (prep(b, i), k_blk + h)),
            pl.BlockSpec((rows, vw), lambda b, h, i: (prep(b, i), v_blk + h)),
            pl.BlockSpec((rows, vw), lambda b, h, i: (solve(b, i), z_blk + h)),
            pl.BlockSpec((GDN_CONV, qw), lambda b, h, i: (0, h)),
            pl.BlockSpec((GDN_CONV, qw), lambda b, h, i: (0, k_blk + h)),
            pl.BlockSpec((GDN_CONV, vw), lambda b, h, i: (0, v_blk + h)),
            pl.BlockSpec((128, rows), lambda b, h, i: (0, prep(b, i))),
            pl.BlockSpec((1, dh), lambda b, h, i: (0, 0)),
        ],
        out_specs=pl.BlockSpec((rows, vw), lambda b, h, i: (solve(b, i), h)),
        scratch_shapes=[
            pltpu.VMEM((GDN_HALO, qw), F32),
            pltpu.VMEM((GDN_HALO, qw), F32),
            pltpu.VMEM((GDN_HALO, vw), F32),
            pltpu.VMEM((128, rows), F32),
            pltpu.VMEM((rows, qw), F32),
            pltpu.VMEM((rows, qw), F32),
            pltpu.VMEM((rows, vw), F32),
            pltpu.VMEM((8, rows), F32),
            pltpu.VMEM((rows, 128), F32),
            pltpu.VMEM((GDN_VH, dh, dh), F32),
        ],
        compiler_params=pltpu.CompilerParams(
            dimension_semantics=("parallel", "parallel", "arbitrary"), vmem_limit_bytes=VMEM_LIMIT),
        name="gdn_chunk",
    )(a_log, dt_bias, proj, proj, proj, proj, conv_w, conv_w, conv_w, tail_t, norm_w.reshape(1, dh))


def kernel(x, ln_g, ln_b, ffn_w_up, ffn_w_down, gdn_w_in, gdn_conv_w, gdn_a_log, gdn_dt_bias,
           gdn_norm_w, gdn_w_out, gla_w_in, gla_w_gate, gla_gate_bias, gla_norm_w, gla_w_out,
           s5_a_re, s5_a_im, s5_b_re, s5_b_im, s5_c_re, s5_c_im, s5_d, s5_log_dt, s5_w_glu):
    bsz, s, d = x.shape
    h = x.reshape(bsz * s, d)
    w_up = ffn_w_up.astype(BF16)
    w_down = ffn_w_down.astype(BF16)
    for i in range(DEPTH):
        kind, j = i % 3, i // 3
        s5_layer = kind == 2
        h = _ffn_ln(h, w_up[i, 0], w_down[i, 0], ln_g[i, 0], ln_b[i, 0], s, out_time_major=s5_layer)
        if kind == 0:
            n_main = GDN_CONV_DIM + GDN_V_DIM
            w_in = gdn_w_in[j].astype(BF16)
            proj = _matmul(h, w_in, n_main)
            tail = _matmul(h, jnp.pad(w_in[:, n_main:], ((0, 0), (0, 128 - 2 * GDN_V_HEADS))))
            o = _gdn_core(proj, tail, gdn_conv_w[j], gdn_a_log[j], gdn_dt_bias[j], gdn_norm_w[j], bsz, s)
            h = _matmul_res_ln(o, gdn_w_out[j].astype(BF16), h, ln_g[i, 1], ln_b[i, 1])
        elif kind == 1:
            n_main = 2 * GLA_K_DIM + 2 * GLA_V_DIM
            w_in = gla_w_in[j].astype(BF16)
            proj = _matmul(h, w_in, n_main)
            tail = _matmul(h, jnp.pad(w_in[:, n_main:], ((0, 0), (0, 128 - GLA_GATE_RANK))))
            o = _gla_core(proj, tail, gla_w_gate[j], gla_gate_bias[j], gla_norm_w[j], bsz, s)
            h = _matmul_res_ln(o, gla_w_out[j].astype(BF16), h, ln_g[i, 1], ln_b[i, 1])
        else:
            ht = h.reshape(s * bsz, d)
            y = _s5_core(ht, s5_a_re[j], s5_a_im[j], s5_b_re[j], s5_b_im[j], s5_c_re[j], s5_c_im[j],
                         s5_d[j], s5_log_dt[j])
            ht = _glu_res_ln(y, s5_w_glu[j].astype(BF16), ht, ln_g[i, 1], ln_b[i, 1])
            h = ht.reshape(s, bsz * d)
        h = _ffn_ln(h, w_up[i, 1], w_down[i, 1], ln_g[i, 2], ln_b[i, 2], s, x_time_major=s5_layer)
    return h.reshape(bsz, s, d)
```

```python
import jax
import jax.numpy as jnp
from jax import lax
from jax.experimental import pallas as pl
from jax.experimental.pallas import tpu as pltpu

F32 = jnp.float32
BF16 = jnp.bfloat16

D_MODEL = 2048
DEPTH = 4
ALPHA = (2.0 * DEPTH) ** 0.25
LN_EPS = 1e-5
RMS_EPS = 1e-6

GDN_QK_HEADS = 16
GDN_V_HEADS = 32
GDN_HEAD_DIM = 128
GDN_CONV = 4
GDN_CHUNK = 64
GDN_QK_DIM = GDN_QK_HEADS * GDN_HEAD_DIM
GDN_V_DIM = GDN_V_HEADS * GDN_HEAD_DIM
GDN_CONV_DIM = 2 * GDN_QK_DIM + GDN_V_DIM

GLA_HEADS = 4
GLA_K_DIM = D_MODEL // 2
GLA_V_DIM = D_MODEL
GLA_HEAD_K = GLA_K_DIM // GLA_HEADS
GLA_HEAD_V = GLA_V_DIM // GLA_HEADS
GLA_GATE_RANK = 16
GLA_GATE_TAU = 16.0
GLA_CHUNK = 64

S5_GROUP = 16
S5_GROUPS = D_MODEL // S5_GROUP
S5_STATE = 64

VMEM_LIMIT = 48 * 1024 * 1024


def _layer_norm(y, g, b):
    mu = jnp.mean(y, axis=-1, keepdims=True)
    yc = y - mu
    var = jnp.mean(yc * yc, axis=-1, keepdims=True)
    return yc * lax.rsqrt(var + LN_EPS) * g + b


def _softplus(x):
    return jnp.maximum(x, 0.0) + jnp.log1p(jnp.exp(-jnp.abs(x)))


def _log_sigmoid(z):
    return jnp.minimum(z, 0.0) - jnp.log1p(jnp.exp(-jnp.abs(z)))


def _silu(x):
    return x * jax.nn.sigmoid(x)


def _dot(a, b):
    return jnp.dot(a, b, preferred_element_type=F32)


def _dot_tn(a, b):
    return lax.dot_general(a, b, (((0,), (0,)), ((), ())), preferred_element_type=F32)


def _dot_nt(a, b):
    return lax.dot_general(a, b, (((1,), (1,)), ((), ())), preferred_element_type=F32)


def _split3(x):
    hi = x.astype(BF16)
    r1 = x - hi.astype(F32)
    mid = r1.astype(BF16)
    lo = (r1 - mid.astype(F32)).astype(BF16)
    return hi, mid, lo


def _ffn_kernel(x_ref, wg_ref, wu_ref, wd_ref, g_ref, b_ref, o_ref, xb_ref, acc_ref):
    f = pl.program_id(1)

    @pl.when(f == 0)
    def _():
        xb_ref[...] = x_ref[...].astype(BF16)
        acc_ref[...] = jnp.zeros_like(acc_ref)

    xb = xb_ref[...]
    gate = _dot(xb, wg_ref[...])
    up = _dot(xb, wu_ref[...])
    act = (_silu(gate) * up).astype(BF16)
    acc_ref[...] += _dot(act, wd_ref[...])

    @pl.when(f == pl.num_programs(1) - 1)
    def _():
        y = ALPHA * x_ref[...] + 0.5 * acc_ref[...]
        o_ref[...] = _layer_norm(y, g_ref[...], b_ref[...])


def _ffn_ln(x, w_up, w_down, layer, half, g, b, seq, *, x_time_major=False, out_time_major=False,
            tm=512, tf=512):
    fdim, d = w_down.shape[2:]
    t = x.size // d
    bsz = t // seq
    nf = fdim // tf
    nsb = seq // tm

    def rows_map(time_major):
        if time_major:
            return lambda i, f: (i % nsb, i // nsb)
        return lambda i, f: (i, 0)

    out_shape = (seq, bsz * d) if out_time_major else (t, d)
    return pl.pallas_call(
        _ffn_kernel,
        out_shape=jax.ShapeDtypeStruct(out_shape, F32),
        grid=(t // tm, nf),
        in_specs=[
            pl.BlockSpec((tm, d), rows_map(x_time_major)),
            pl.BlockSpec((None, None, d, tf), lambda i, f: (layer, half, 0, f)),
            pl.BlockSpec((None, None, d, tf), lambda i, f: (layer, half, 0, f + nf)),
            pl.BlockSpec((None, None, tf, d), lambda i, f: (layer, half, f, 0)),
            pl.BlockSpec((1, d), lambda i, f: (0, 0)),
            pl.BlockSpec((1, d), lambda i, f: (0, 0)),
        ],
        out_specs=pl.BlockSpec((tm, d), rows_map(out_time_major)),
        scratch_shapes=[pltpu.VMEM((tm, d), BF16), pltpu.VMEM((tm, d), F32)],
        compiler_params=pltpu.CompilerParams(
            dimension_semantics=("parallel", "arbitrary"), vmem_limit_bytes=VMEM_LIMIT),
        name="ffn_ln",
    )(x, w_up, w_up, w_down, g.reshape(1, d), b.reshape(1, d))


def _proj_kernel(x_ref, w_ref, wt_ref, o_ref, ot_ref):
    xb = x_ref[...].astype(BF16)
    o_ref[...] = _dot(xb, w_ref[...])

    @pl.when(pl.program_id(1) == 0)
    def _():
        ot_ref[...] = _dot(xb, wt_ref[...])


def _in_proj(x, w, n, w_tail, *, tm=1024, tn=1024):
    t, k = x.shape
    nt = w_tail.shape[1]
    return pl.pallas_call(
        _proj_kernel,
        out_shape=(jax.ShapeDtypeStruct((t, n), F32), jax.ShapeDtypeStruct((t, nt), F32)),
        grid=(t // tm, n // tn),
        in_specs=[
            pl.BlockSpec((tm, k), lambda i, j: (i, 0)),
            pl.BlockSpec((k, tn), lambda i, j: (0, j)),
            pl.BlockSpec((k, nt), lambda i, j: (0, 0)),
        ],
        out_specs=(pl.BlockSpec((tm, tn), lambda i, j: (i, j)),
                   pl.BlockSpec((tm, nt), lambda i, j: (i, 0))),
        compiler_params=pltpu.CompilerParams(
            dimension_semantics=("parallel", "arbitrary"), vmem_limit_bytes=VMEM_LIMIT),
        name="proj",
    )(x, w, w_tail)


def _mm_res_ln_kernel(a_ref, w_ref, r_ref, g_ref, b_ref, o_ref):
    y = ALPHA * r_ref[...] + _dot(a_ref[...], w_ref[...])
    o_ref[...] = _layer_norm(y, g_ref[...], b_ref[...])


def _matmul_res_ln(a, w, res, g, b, *, tm=512):
    t, kdim = a.shape
    d = w.shape[1]
    return pl.pallas_call(
        _mm_res_ln_kernel,
        out_shape=jax.ShapeDtypeStruct((t, d), F32),
        grid=(t // tm,),
        in_specs=[
            pl.BlockSpec((tm, kdim), lambda i: (i, 0)),
            pl.BlockSpec((kdim, d), lambda i: (0, 0), pipeline_mode=pl.Buffered(1)),
            pl.BlockSpec((tm, d), lambda i: (i, 0)),
            pl.BlockSpec((1, d), lambda i: (0, 0)),
            pl.BlockSpec((1, d), lambda i: (0, 0)),
        ],
        out_specs=pl.BlockSpec((tm, d), lambda i: (i, 0)),
        compiler_params=pltpu.CompilerParams(
            dimension_semantics=("parallel",), vmem_limit_bytes=VMEM_LIMIT),
        name="out_proj_ln",
    )(a, w, res, g.reshape(1, d), b.reshape(1, d))


def _glu_res_ln_kernel(a_ref, wv_ref, wg_ref, r_ref, g_ref, b_ref, o_ref):
    a = a_ref[...]
    val = _dot(a, wv_ref[...])
    gate = _dot(a, wg_ref[...])
    y = ALPHA * r_ref[...] + val * jax.nn.sigmoid(gate)
    o_ref[...] = _layer_norm(y, g_ref[...], b_ref[...])


def _glu_res_ln(a, w_glu, res, g, b, *, tm=512):
    t, kdim = a.shape
    d = res.shape[1]
    once = pl.Buffered(1)
    return pl.pallas_call(
        _glu_res_ln_kernel,
        out_shape=jax.ShapeDtypeStruct((t, d), F32),
        grid=(t // tm,),
        in_specs=[
            pl.BlockSpec((tm, kdim), lambda i: (i, 0)),
            pl.BlockSpec((kdim, d), lambda i: (0, 0), pipeline_mode=once),
            pl.BlockSpec((kdim, d), lambda i: (0, 1), pipeline_mode=once),
            pl.BlockSpec((tm, d), lambda i: (i, 0)),
            pl.BlockSpec((1, d), lambda i: (0, 0)),
            pl.BlockSpec((1, d), lambda i: (0, 0)),
        ],
        out_specs=pl.BlockSpec((tm, d), lambda i: (i, 0)),
        compiler_params=pltpu.CompilerParams(
            dimension_semantics=("parallel",), vmem_limit_bytes=VMEM_LIMIT),
        name="glu_ln",
    )(a, w_glu, w_glu, res, g.reshape(1, d), b.reshape(1, d))


S5_GB = 16
S5_NGB = S5_GROUPS // S5_GB
S5_SB = S5_GB * S5_STATE
S5_BATCH = 4
S5_SUB = 128


def _s5_discretize_kernel(are_ref, aim_ref, ldt_ref, abr_ref, abi_ref, a2r_ref, a2i_ref, cfr_ref, cfi_ref):
    lr, li = are_ref[...], aim_ref[...]
    dt = jnp.exp(ldt_ref[...])
    mag = jnp.exp(lr * dt)
    ar = mag * jnp.cos(li * dt)
    ai = mag * jnp.sin(li * dt)
    abr_ref[...] = ar
    abi_ref[...] = ai
    a2r_ref[...] = ar * ar - ai * ai
    a2i_ref[...] = 2.0 * ar * ai
    nr, ni = ar - 1.0, ai
    den = lr * lr + li * li
    cfr_ref[...] = (nr * lr + ni * li) / den
    cfi_ref[...] = (ni * lr - nr * li) / den


def _s5_scan_kernel(u_ref, bm_ref, cm_ref, d_ref, a1r_ref, a1i_ref, a0r_ref, a0i_ref, o_ref,
                    hr_ref, hi_ref):
    rows = u_ref.shape[0]

    @pl.when(pl.program_id(1) == 0)
    def _():
        hr_ref[...] = jnp.zeros_like(hr_ref)
        hi_ref[...] = jnp.zeros_like(hi_ref)

    a1r, a1i = a1r_ref[0], a1i_ref[0]
    a0r, a0i = a0r_ref[0], a0i_ref[0]
    hi_rows = lax.broadcasted_iota(jnp.int32, (8, S5_SB), 0) >= S5_BATCH

    pr, pi = hr_ref[...], hi_ref[...]
    for r0 in range(0, rows, S5_SUB):
        u = u_ref[r0:r0 + S5_SUB, :]
        bu = _dot(u.astype(BF16), bm_ref[0])
        hs_r, hs_i = [], []
        for k in range(S5_SUB // 8):
            xr = bu[8 * k:8 * k + 8, 0:S5_SB]
            xi = bu[8 * k:8 * k + 8, S5_SB:2 * S5_SB]
            sr = pltpu.roll(xr, S5_BATCH, 0)
            si = pltpu.roll(xi, S5_BATCH, 0)
            hr = xr + (a0r * sr - a0i * si) + (a1r * pr - a1i * pi)
            hi = xi + (a0r * si + a0i * sr) + (a1r * pi + a1i * pr)
            hs_r.append(hr)
            hs_i.append(hi)
            pr = jnp.where(hi_rows, hr, pltpu.roll(hr, S5_BATCH, 0))
            pi = jnp.where(hi_rows, hi, pltpu.roll(hi, S5_BATCH, 0))
        states = jnp.concatenate([jnp.concatenate(hs_r, axis=0), jnp.concatenate(hs_i, axis=0)], axis=1)
        y = _dot(states.astype(BF16), cm_ref[0]) + d_ref[...] * u
        o_ref[r0:r0 + S5_SUB, :] = jax.nn.gelu(y).astype(o_ref.dtype)
    hr_ref[...] = pr
    hi_ref[...] = pi


def _s5_core(xt, a_re, a_im, b_re, b_im, c_re, c_im, d_skip, log_dt, *, ts=128):
    t, d = xt.shape
    rows = ts * S5_BATCH
    sds = jax.ShapeDtypeStruct((S5_GROUPS, S5_STATE), F32)
    abr, abi, a2r, a2i, cfr, cfi = pl.pallas_call(
        _s5_discretize_kernel, out_shape=(sds,) * 6, name="s5_discretize",
    )(a_re, a_im, log_dt.reshape(S5_GROUPS, 1))

    bbr = cfr[..., None] * b_re - cfi[..., None] * b_im
    bbi = cfr[..., None] * b_im + cfi[..., None] * b_re
    eye = jnp.eye(S5_GB, dtype=F32)

    def blockdiag_in(m):
        m = m.reshape(S5_NGB, S5_GB, S5_STATE, S5_GROUP)
        return jnp.einsum("ngph,gk->nghkp", m, eye).reshape(S5_NGB, S5_GB * S5_GROUP, S5_SB)

    def blockdiag_out(m):
        m = m.reshape(S5_NGB, S5_GB, S5_GROUP, S5_STATE)
        return jnp.einsum("nghp,gk->ngpkh", m, eye).reshape(S5_NGB, S5_SB, S5_GB * S5_GROUP)

    bmat = jnp.concatenate([blockdiag_in(bbr), blockdiag_in(bbi)], axis=-1).astype(BF16)
    cmat = jnp.concatenate([blockdiag_out(c_re), blockdiag_out(-c_im)], axis=1).astype(BF16)

    def tile8(top, bot):
        top = jnp.broadcast_to(top.reshape(S5_NGB, 1, S5_SB), (S5_NGB, S5_BATCH, S5_SB))
        bot = jnp.broadcast_to(bot.reshape(S5_NGB, 1, S5_SB), (S5_NGB, S5_BATCH, S5_SB))
        return jnp.concatenate([top, bot], axis=1)

    zero = jnp.zeros_like(abr)
    a1r, a1i = tile8(abr, a2r), tile8(abi, a2i)
    a0r, a0i = tile8(zero, abr), tile8(zero, abi)

    kin = S5_GB * S5_GROUP
    coef_spec = pl.BlockSpec((1, 8, S5_SB), lambda g, i: (g, 0, 0))
    return pl.pallas_call(
        _s5_scan_kernel,
        out_shape=jax.ShapeDtypeStruct((t, d), BF16),
        grid=(S5_NGB, t // rows),
        in_specs=[
            pl.BlockSpec((rows, kin), lambda g, i: (i, g)),
            pl.BlockSpec((1, kin, 2 * S5_SB), lambda g, i: (g, 0, 0)),
            pl.BlockSpec((1, 2 * S5_SB, kin), lambda g, i: (g, 0, 0)),
            pl.BlockSpec((1, kin), lambda g, i: (0, g)),
            coef_spec, coef_spec, coef_spec, coef_spec,
        ],
        out_specs=pl.BlockSpec((rows, kin), lambda g, i: (i, g)),
        scratch_shapes=[pltpu.VMEM((8, S5_SB), F32), pltpu.VMEM((8, S5_SB), F32)],
        compiler_params=pltpu.CompilerParams(
            dimension_semantics=("parallel", "arbitrary"), vmem_limit_bytes=VMEM_LIMIT),
        name="s5_scan",
    )(xt, bmat, cmat, d_skip.reshape(1, d), a1r, a1i, a0r, a0i)


GLA_ROWS = 512
GLA_SUB = 16


def _gla_kernel(q_ref, k_ref, v_ref, r_ref, gl_ref, wg_ref, gb_ref, nw_ref, o_ref, st_ref, f_ref, b_ref):
    c, sc, dk, dv = GLA_CHUNK, GLA_SUB, GLA_HEAD_K, GLA_HEAD_V
    heads = range(GLA_HEADS)

    @pl.when(pl.program_id(1) == 0)
    def _():
        st_ref[...] = jnp.zeros_like(st_ref)

    g_hi, g_mid, _ = _split3(gl_ref[...])
    w_hi, w_mid, _ = _split3(wg_ref[...])
    z = _dot(g_hi, w_hi) + (_dot(g_hi, w_mid) + _dot(g_mid, w_hi)) + gb_ref[...]
    f_ref[...] = _log_sigmoid(z) * (1.0 / GLA_GATE_TAU)

    row = lax.broadcasted_iota(jnp.int32, (c, c), 0)
    col = lax.broadcasted_iota(jnp.int32, (c, c), 1)
    tril = (row >= col).astype(BF16)
    sub = lax.broadcasted_iota(jnp.int32, (8, GLA_HEAD_K), 0)
    lane = lax.broadcasted_iota(jnp.int32, (8, 128), 1)

    def chunk(ci, carry):
        r0 = pl.multiple_of(ci * c, c)
        bc = sum(_dot(tril, part) for part in _split3(f_ref[pl.ds(r0, c), :]))
        b_ref[...] = bc

        tiles = [[] for _ in heads]
        for si in range(c // sc):
            base = si * sc
            qs = [q_ref[pl.ds(r0 + base, sc), h * dk:(h + 1) * dk] * (dk ** -0.5) for h in heads]
            bs = [b_ref[base:base + sc, h * dk:(h + 1) * dk] for h in heads]
            if si == 0:
                off = [jnp.zeros((sc, 128), F32) for _ in heads]
            else:
                off = []
                for h in heads:
                    ref_b = b_ref[base - 1:base, h * dk:(h + 1) * dk]
                    qh = (qs[h] * jnp.exp(bs[h] - ref_b)).astype(BF16)
                    kh = (k_ref[pl.ds(r0, base), h * dk:(h + 1) * dk]
                          * jnp.exp(ref_b - b_ref[0:base, h * dk:(h + 1) * dk])).astype(BF16)
                    kh = jnp.concatenate([kh, jnp.zeros((128 - base, dk), BF16)], axis=0)
                    off.append(_dot_nt(qh, kh))
            for h in heads:
                for t2 in range(sc // 8):
                    qi = qs[h][8 * t2:8 * t2 + 8, :]
                    bi = bs[h][8 * t2:8 * t2 + 8, :]
                    acc = off[h][8 * t2:8 * t2 + 8, :]
                    for jj in range(8 * (t2 + 1)):
                        j = base + jj
                        kj = k_ref[pl.ds(r0 + j, 1), h * dk:(h + 1) * dk]
                        bj = b_ref[j:j + 1, h * dk:(h + 1) * dk]
                        p = qi * kj * jnp.exp(bi - bj)
                        if jj >= 8 * t2:
                            p = jnp.where(sub >= jj - 8 * t2, p, 0.0)
                        s = jnp.sum(p, axis=1, keepdims=True)
                        acc = jnp.where(lane == j, s, acc)
                    tiles[h].append(acc)
        attn = [jnp.concatenate(tiles[h], axis=0)[:, :c].astype(BF16) for h in heads]

        blast = bc[c - 1:c, :]
        qe = (q_ref[pl.ds(r0, c), :] * (dk ** -0.5) * jnp.exp(bc)).astype(BF16)
        ke = (k_ref[pl.ds(r0, c), :] * jnp.exp(blast - bc)).astype(BF16)
        eb = jnp.exp(blast)
        st = [st_ref[h] for h in heads]
        v = [v_ref[pl.ds(r0, c), h * dv:(h + 1) * dv].astype(BF16) for h in heads]
        o = [_dot_nt(qe[:, h * dk:(h + 1) * dk], st[h].astype(BF16)) + _dot(attn[h], v[h]) for h in heads]
        for h in heads:
            st_ref[h] = st[h] * eb[:, h * dk:(h + 1) * dk] + _dot_tn(v[h], ke[:, h * dk:(h + 1) * dk])
        for h in heads:
            oh = o[h] * lax.rsqrt(jnp.mean(o[h] * o[h], axis=-1, keepdims=True) + RMS_EPS) * nw_ref[...]
            gate = _silu(r_ref[pl.ds(r0, c), h * dv:(h + 1) * dv])
            o_ref[pl.ds(r0, c), h * dv:(h + 1) * dv] = (oh * gate).astype(o_ref.dtype)
        return carry

    lax.fori_loop(0, q_ref.shape[0] // c, chunk, 0)


def _gla_core(proj, tail, w_gate, gate_bias, norm_w, bsz, s):
    t = bsz * s
    rows = min(GLA_ROWS, s)
    nblk = s // rows
    wg = jnp.pad(w_gate, ((0, 128 - GLA_GATE_RANK), (0, 0)))
    return pl.pallas_call(
        _gla_kernel,
        out_shape=jax.ShapeDtypeStruct((t, GLA_V_DIM), BF16),
        grid=(bsz, nblk),
        in_specs=[
            pl.BlockSpec((rows, GLA_K_DIM), lambda b, i: (b * nblk + i, 0)),
            pl.BlockSpec((rows, GLA_K_DIM), lambda b, i: (b * nblk + i, 1)),
            pl.BlockSpec((rows, GLA_V_DIM), lambda b, i: (b * nblk + i, 1)),
            pl.BlockSpec((rows, GLA_V_DIM), lambda b, i: (b * nblk + i, 2)),
            pl.BlockSpec((rows, 128), lambda b, i: (b * nblk + i, 0)),
            pl.BlockSpec((128, GLA_K_DIM), lambda b, i: (0, 0)),
            pl.BlockSpec((1, GLA_K_DIM), lambda b, i: (0, 0)),
            pl.BlockSpec((1, GLA_HEAD_V), lambda b, i: (0, 0)),
        ],
        out_specs=pl.BlockSpec((rows, GLA_V_DIM), lambda b, i: (b * nblk + i, 0)),
        scratch_shapes=[pltpu.VMEM((GLA_HEADS, GLA_HEAD_V, GLA_HEAD_K), F32),
                        pltpu.VMEM((rows, GLA_K_DIM), F32),
                        pltpu.VMEM((GLA_CHUNK, GLA_K_DIM), F32)],
        compiler_params=pltpu.CompilerParams(
            dimension_semantics=("parallel", "arbitrary"), vmem_limit_bytes=VMEM_LIMIT),
        name="gla_chunk",
    )(proj, proj, proj, proj, tail, wg, gate_bias.reshape(1, GLA_K_DIM), norm_w.reshape(1, GLA_HEAD_V))


GDN_ROWS = 256
GDN_QH = 4
GDN_VH = 2 * GDN_QH
GDN_HALO = 8


def _gdn_kernel(alog_ref, dtb_ref, q_ref, k_ref, v_ref, z_ref, cwq_ref, cwk_ref, cwv_ref, gt_ref, nw_ref,
                o_ref, xq_ref, xk_ref, xv_ref, rp_ref, qn_ref, kn_ref, vc_ref, gr_ref, cp_ref, s_ref):
    rows, c, dh, nv = GDN_ROWS, GDN_CHUNK, GDN_HEAD_DIM, GDN_VH
    nchunk = rows // c
    hp = pl.program_id(1)
    heads = range(nv)

    @pl.when(pl.program_id(2) == 0)
    def _():
        for ref in (s_ref, xq_ref, xk_ref, xv_ref, qn_ref, kn_ref, vc_ref, gr_ref, cp_ref):
            ref[...] = jnp.zeros_like(ref)

    ri = lax.broadcasted_iota(jnp.int32, (rows, rows), 0)
    ci = lax.broadcasted_iota(jnp.int32, (rows, rows), 1)
    same = (ri >> 6) == (ci >> 6)

    def conv_silu(x_ref, xs_ref, cw_ref, lo, width):
        cur = x_ref[:, lo:lo + width].reshape(rows // 8, 8, width)
        prev = jnp.concatenate([xs_ref[:, lo:lo + width].reshape(1, 8, width), cur[:-1]], axis=0)
        sub = lax.broadcasted_iota(jnp.int32, (rows // 8, 8, width), 1)
        acc = cur * cw_ref[GDN_CONV - 1:GDN_CONV, lo:lo + width]
        for s in range(1, GDN_CONV):
            shifted = pltpu.roll(jnp.where(sub >= 8 - s, prev, cur), s, 1)
            acc = acc + shifted * cw_ref[GDN_CONV - 1 - s:GDN_CONV - s, lo:lo + width]
        xs_ref[:, lo:lo + width] = x_ref[rows - GDN_HALO:rows, lo:lo + width]
        return _silu(acc).reshape(rows, width)

    def l2n(x):
        return x * lax.rsqrt(jnp.sum(x * x, axis=-1, keepdims=True) + RMS_EPS)

    nxt = {"qn": [], "kn": [], "vc": []}

    def stage_a():
        g_rows, beta_rows = [], []
        for e in heads:
            h = nv * hp + e
            b_row = gt_ref[pl.ds(h, 1), :]
            a_row = gt_ref[pl.ds(GDN_V_HEADS + h, 1), :]
            rate = jnp.exp(jnp.full((1, rows), alog_ref[h], F32))
            g_rows.append(-rate * _softplus(a_row + dtb_ref[h]))
            beta_rows.append(jax.nn.sigmoid(b_row))
        g8 = jnp.concatenate(g_rows + [jnp.zeros((1, rows), F32)] * (8 - nv), axis=0)
        sum_mats = jnp.concatenate([(same & (ri <= ci)).astype(BF16), same.astype(BF16)], axis=1)
        sums = sum(_dot(part, sum_mats) for part in _split3(g8))
        nxt["gc"] = sums[:, :rows]
        rp_ref[...] = jnp.zeros_like(rp_ref)
        rp_ref[0:8, :] = sums[:, :rows]
        for e in heads:
            rp_ref[8 + e:9 + e, :] = beta_rows[e]
        rp_ref[16:24, :] = sums[:, rows:]
        yield
        for i in range(GDN_QH):
            nxt["qn"].append(l2n(conv_silu(q_ref, xq_ref, cwq_ref, i * dh, dh)) * (dh ** -0.5))
            yield
            nxt["kn"].append(l2n(conv_silu(k_ref, xk_ref, cwk_ref, i * dh, dh)))
            yield
        for e in heads:
            nxt["vc"].append(conv_silu(v_ref, xv_ref, cwv_ref, e * dh, dh))
            yield

    prepare = stage_a()

    def advance():
        next(prepare, None)

    qn = [qn_ref[:, i * dh:(i + 1) * dh] for i in range(GDN_QH)]
    kn = [kn_ref[:, i * dh:(i + 1) * dh] for i in range(GDN_QH)]
    vc = vc_ref[...]
    gc8 = gr_ref[...]

    causal = same & (ri >= ci)
    strict = same & (ri > ci)
    in16 = ((ri >> 4) == (ci >> 4)) & (ri > ci)
    off32 = ((ri >> 5) == (ci >> 5)) & ((ri >> 4) > (ci >> 4))
    off64 = same & ((ri >> 5) > (ci >> 5))
    eye = (ri == ci).astype(F32)

    knb = [x.astype(BF16) for x in kn]
    kk = [_dot_nt(x, x) for x in knb]
    qk = [_dot_nt(qn[i].astype(BF16), knb[i]) for i in range(GDN_QH)]

    gc_col = [cp_ref[:, e:e + 1] for e in heads]
    beta_col = [cp_ref[:, 8 + e:9 + e] for e in heads]
    gt_col = [cp_ref[:, 16 + e:17 + e] for e in heads]
    decay = [jnp.where(causal, jnp.exp(gc_col[e] - gc8[e:e + 1, :]), 0.0) for e in heads]
    attn = [(qk[e // 2] * decay[e]).astype(BF16) for e in heads]
    lmat = [jnp.where(strict, kk[e // 2] * decay[e], 0.0) * beta_col[e] for e in heads]
    rhs = [jnp.concatenate([vc[:, e * dh:(e + 1) * dh] * beta_col[e],
                            kn[e // 2] * (beta_col[e] * jnp.exp(gc_col[e]))], axis=1).astype(BF16)
           for e in heads]

    y32 = [-jnp.where(in16, lm, 0.0) for lm in lmat]
    ts = [eye + y for y in y32]
    ys = [y.astype(BF16) for y in y32]
    for _ in range(3):
        ys = [_dot(y, y).astype(BF16) for y in ys]
        advance()
        ts = [t + _dot(t.astype(BF16), y) for t, y in zip(ts, ys)]
        advance()
    for off in (off32, off64):
        lo = [jnp.where(off, lm, 0.0).astype(BF16) for lm in lmat]
        tb = [t.astype(BF16) for t in ts]
        tl = [_dot(t, l).astype(BF16) for t, l in zip(tb, lo)]
        advance()
        ts = [t - _dot(a, b) for t, a, b in zip(ts, tl, tb)]
        advance()
    sol = [_dot(t.astype(BF16), r).astype(BF16) for t, r in zip(ts, rhs)]
    advance()

    awu = [_dot(a, s) for a, s in zip(attn, sol)]
    advance()
    qeff = [(qn[e // 2] * jnp.exp(gc_col[e]) - awu[e][:, dh:]).astype(BF16) for e in heads]
    kt = [(kn[e // 2] * jnp.exp(gt_col[e] - gc_col[e])).astype(BF16) for e in heads]
    kwu = [[_dot_tn(kt[e][j * c:(j + 1) * c, :], sol[e][j * c:(j + 1) * c, :]) for e in heads]
           for j in range(nchunk)]

    state = [s_ref[e] for e in heads]
    outs = [[] for _ in heads]
    for j in range(nchunk):
        sb = [s.astype(BF16) for s in state]
        for e in heads:
            outs[e].append(_dot(qeff[e][j * c:(j + 1) * c, :], sb[e]) + awu[e][j * c:(j + 1) * c, :dh])
        state = [state[e] * jnp.exp(gt_col[e][j * c:j * c + 1, :]) + kwu[j][e][:, :dh]
                 - _dot(kwu[j][e][:, dh:].astype(BF16), sb[e]) for e in heads]
    for e in heads:
        s_ref[e] = state[e]
        o = jnp.concatenate(outs[e], axis=0)
        o = o * lax.rsqrt(jnp.mean(o * o, axis=-1, keepdims=True) + RMS_EPS) * nw_ref[...]
        o_ref[:, e * dh:(e + 1) * dh] = (o * _silu(z_ref[:, e * dh:(e + 1) * dh])).astype(o_ref.dtype)

    for _ in prepare:
        pass
    qn_ref[...] = jnp.concatenate(nxt["qn"], axis=1)
    kn_ref[...] = jnp.concatenate(nxt["kn"], axis=1)
    vc_ref[...] = jnp.concatenate(nxt["vc"], axis=1)
    gr_ref[...] = nxt["gc"]
    cp_ref[...] = rp_ref[...].T


def _gdn_core(proj, tail, conv_w, a_log, dt_bias, norm_w, bsz, s):
    t = bsz * s
    rows, dh = GDN_ROWS, GDN_HEAD_DIM
    qw, vw = GDN_QH * dh, GDN_VH * dh
    nblk = s // rows
    k_blk = GDN_QK_DIM // qw
    v_blk = 2 * GDN_QK_DIM // vw
    z_blk = GDN_CONV_DIM // vw
    tail_t = tail.T
    smem = pl.BlockSpec(memory_space=pltpu.SMEM)

    def prep(b, i):
        return b * nblk + jnp.minimum(i, nblk - 1)

    def solve(b, i):
        return b * nblk + jnp.maximum(i - 1, 0)

    return pl.pallas_call(
        _gdn_kernel,
        out_shape=jax.ShapeDtypeStruct((t, GDN_V_DIM), BF16),
        grid=(bsz, GDN_QK_HEADS // GDN_QH, nblk + 1),
        in_specs=[
            smem, smem,
            pl.BlockSpec((rows, qw), lambda b, h, i: (prep(b, i), h)),
            pl.BlockSpec((rows, qw), lambda b, h, i: (prep(b, i), k_blk + h)),
            pl.BlockSpec((rows, vw), lambda b, h, i: (prep(b, i), v_blk + h)),
            pl.BlockSpec((rows, vw), lambda b, h, i: (solve(b, i), z_blk + h)),
            pl.BlockSpec((GDN_CONV, qw), lambda b, h, i: (0, h)),
            pl.BlockSpec((GDN_CONV, qw), lambda b, h, i: (0, k_blk + h)),
            pl.BlockSpec((GDN_CONV, vw), lambda b, h, i: (0, v_blk + h)),
            pl.BlockSpec((128, rows), lambda b, h, i: (0, prep(b, i))),
            pl.BlockSpec((1, dh), lambda b, h, i: (0, 0)),
        ],
        out_specs=pl.BlockSpec((rows, vw), lambda b, h, i: (solve(b, i), h)),
        scratch_shapes=[
            pltpu.VMEM((GDN_HALO, qw), F32),
            pltpu.VMEM((GDN_HALO, qw), F32),
            pltpu.VMEM((GDN_HALO, vw), F32),
            pltpu.VMEM((128, rows), F32),
            pltpu.VMEM((rows, qw), F32),
            pltpu.VMEM((rows, qw), F32),
            pltpu.VMEM((rows, vw), F32),
            pltpu.VMEM((8, rows), F32),
            pltpu.VMEM((rows, 128), F32),
            pltpu.VMEM((GDN_VH, dh, dh), F32),
        ],
        compiler_params=pltpu.CompilerParams(
            dimension_semantics=("parallel", "parallel", "arbitrary"), vmem_limit_bytes=VMEM_LIMIT),
        name="gdn_chunk",
    )(a_log, dt_bias, proj, proj, proj, proj, conv_w, conv_w, conv_w, tail_t, norm_w.reshape(1, dh))


def kernel(x, ln_g, ln_b, ffn_w_up, ffn_w_down, gdn_w_in, gdn_conv_w, gdn_a_log, gdn_dt_bias,
           gdn_norm_w, gdn_w_out, gla_w_in, gla_w_gate, gla_gate_bias, gla_norm_w, gla_w_out,
           s5_a_re, s5_a_im, s5_b_re, s5_b_im, s5_c_re, s5_c_im, s5_d, s5_log_dt, s5_w_glu):
    bsz, s, d = x.shape
    h = x.reshape(bsz * s, d)
    w_up = ffn_w_up.astype(BF16)
    w_down = ffn_w_down.astype(BF16)
    for i in range(DEPTH):
        kind, j = i % 3, i // 3
        s5_layer = kind == 2
        h = _ffn_ln(h, w_up, w_down, i, 0, ln_g[i, 0], ln_b[i, 0], s, out_time_major=s5_layer)
        if kind == 0:
            n_main = GDN_CONV_DIM + GDN_V_DIM
            w_in = gdn_w_in[j].astype(BF16)
            w_tail = jnp.pad(w_in[:, n_main:], ((0, 0), (0, 128 - 2 * GDN_V_HEADS)))
            proj, tail = _in_proj(h, w_in, n_main, w_tail)
            o = _gdn_core(proj, tail, gdn_conv_w[j], gdn_a_log[j], gdn_dt_bias[j], gdn_norm_w[j], bsz, s)
            h = _matmul_res_ln(o, gdn_w_out[j].astype(BF16), h, ln_g[i, 1], ln_b[i, 1])
        elif kind == 1:
            n_main = 2 * GLA_K_DIM + 2 * GLA_V_DIM
            w_in = gla_w_in[j].astype(BF16)
            w_tail = jnp.pad(w_in[:, n_main:], ((0, 0), (0, 128 - GLA_GATE_RANK)))
            proj, tail = _in_proj(h, w_in, n_main, w_tail)
            o = _gla_core(proj, tail, gla_w_gate[j], gla_gate_bias[j], gla_norm_w[j], bsz, s)
            h = _matmul_res_ln(o, gla_w_out[j].astype(BF16), h, ln_g[i, 1], ln_b[i, 1])
        else:
            ht = h.reshape(s * bsz, d)
            y = _s5_core(ht, s5_a_re[j], s5_a_im[j], s5_b_re[j], s5_b_im[j], s5_c_re[j], s5_c_im[j],
                         s5_d[j], s5_log_dt[j])
            ht = _glu_res_ln(y, s5_w_glu[j].astype(BF16), ht, ln_g[i, 1], ln_b[i, 1])
            h = ht.reshape(s, bsz * d)
        h = _ffn_ln(h, w_up, w_down, i, 1, ln_g[i, 2], ln_b[i, 2], s, x_time_major=s5_layer)
    return h.reshape(bsz, s, d)
```
